```python
import math
import jax, jax.numpy as jnp
from jax import lax
import numpy as np

D_MODEL = 1024
BATCH = 1
SEQ = 16384
DEPTH = 2

ATT_GROUPS = ((128, 1), (512, 4), (2048, 16))
ATT_HEADS_PER_GROUP = 4
ATT_HEAD_DIM = 64
ATT_HEADS = len(ATT_GROUPS) * ATT_HEADS_PER_GROUP
ATT_WIDTH = ATT_HEADS * ATT_HEAD_DIM
ATT_OUT_WIDTH = ATT_HEADS_PER_GROUP * ATT_HEAD_DIM
BAND_BLOCK = 64
ROPE_THETA = 10000.0
DN_HEADS = 6
DN_HEAD_DIM = 128
DN_WIDTH = DN_HEADS * DN_HEAD_DIM
DN_CONV = 5
DN_CHUNK = 64
FFN_DIM = ((8 * D_MODEL // 3 + 127) // 128) * 128
IN_SIZES = (ATT_WIDTH, ATT_WIDTH, ATT_WIDTH, 3 * DN_WIDTH, DN_WIDTH, 2 * DN_HEADS, 2 * DN_HEADS, D_MODEL, D_MODEL)
N_IN = sum(IN_SIZES)
N_SUBLAYERS = 3
EPS = 1e-6
NEG_INF = -1e30

kernel_name = 'hybrid_dilated_attn_gated_deltanet_macaron_block'


def rmsnorm(t, gain):
    t32 = t.astype(jnp.float32)
    y = t32 * lax.rsqrt(jnp.mean(t32 * t32, axis=-1, keepdims=True) + EPS)
    return (y * gain.astype(jnp.float32)).astype(t.dtype)


def ada_pre(t, gain, shift, scale):
    return rmsnorm(t, gain) * (1.0 + scale[:, None, :]) + shift[:, None, :]


def swiglu(h, w_in, w_out):
    gu = h @ w_in
    g, u = jnp.split(gu, 2, axis=-1)
    return (jax.nn.silu(g) * u) @ w_out


def rotary(t, positions):
    e = t.shape[-1]
    half = e // 2
    inv = ROPE_THETA ** (-jnp.arange(half, dtype=jnp.float32) * 2.0 / e)
    ang = positions.astype(jnp.float32)[:, :, None] * inv
    cos = jnp.cos(ang)[:, :, None, :]
    sin = jnp.sin(ang)[:, :, None, :]
    t32 = t.astype(jnp.float32)
    t1, t2 = t32[..., :half], t32[..., half:]
    return jnp.concatenate([t1 * cos - t2 * sin, t2 * cos + t1 * sin], axis=-1).astype(t.dtype)


def dilated_window_attention(q, k, v, window, dilation):
    B, S, H, E = q.shape
    radius = window // (2 * dilation)
    L = S // dilation
    nb = -(-L // BAND_BLOCK)
    pad = nb * BAND_BLOCK - L

    def to_classes(t):
        return t.reshape(B, L, dilation, H, E).transpose(0, 2, 3, 1, 4)

    qb = jnp.pad(to_classes(q), ((0, 0), (0, 0), (0, 0), (0, pad), (0, 0)))
    qb = qb.reshape(B, dilation, H, nb, BAND_BLOCK, E)

    def key_windows(t):
        tp = jnp.pad(to_classes(t), ((0, 0), (0, 0), (0, 0), (BAND_BLOCK, pad + BAND_BLOCK), (0, 0)))
        tp = tp.reshape(B, dilation, H, nb + 2, BAND_BLOCK, E)
        return jnp.concatenate([tp[:, :, :, :-2], tp[:, :, :, 1:-1], tp[:, :, :, 2:]], axis=-2)

    kw = key_windows(k)
    vw = key_windows(v)
    scores = jnp.einsum('bdhnqe,bdhnke->bdhnqk', qb, kw, preferred_element_type=jnp.float32) * (E ** -0.5)
    qi = jnp.arange(BAND_BLOCK)[:, None]
    kj = jnp.arange(3 * BAND_BLOCK)[None, :] - BAND_BLOCK
    kpos = jnp.arange(nb)[:, None, None] * BAND_BLOCK + kj
    valid = (jnp.abs(kj - qi) <= radius) & (kpos >= 0) & (kpos < L)
    scores = jnp.where(valid, scores, NEG_INF)
    m = jnp.max(scores, axis=-1, keepdims=True)
    p = jnp.exp(scores - m)
    den = jnp.sum(p, axis=-1, keepdims=True)
    o = jnp.einsum('bdhnqk,bdhnke->bdhnqe', p, vw.astype(jnp.float32)) / den
    lse = (m + jnp.log(den))[..., 0]
    o = o.reshape(B, dilation, H, nb * BAND_BLOCK, E)[:, :, :, :L]
    o = o.transpose(0, 3, 1, 2, 4).reshape(B, S, H, E)
    lse = lse.reshape(B, dilation, H, nb * BAND_BLOCK)[..., :L]
    lse = lse.transpose(0, 3, 1, 2).reshape(B, S, H)
    return o, lse


def short_conv(t, w):
    C = t.shape[-1]
    pad = w.shape[0] // 2
    return lax.conv_general_dilated(t, w[:, None, :].astype(t.dtype), window_strides=(1,),
                                    padding=[(pad, pad)], dimension_numbers=('NWC', 'WIO', 'NWC'),
                                    feature_group_count=C)


def l2norm(t):
    return t * lax.rsqrt(jnp.sum(t * t, axis=-1, keepdims=True) + EPS)


def gated_delta_chunked(q, k, v, beta, g):
    B, H, S, DK = q.shape
    DV = v.shape[-1]
    C = DN_CHUNK
    N = S // C
    q = q.reshape(B, H, N, C, DK)
    k = k.reshape(B, H, N, C, DK)
    v = v.reshape(B, H, N, C, DV)
    beta = beta.reshape(B, H, N, C)
    G = jnp.cumsum(g.reshape(B, H, N, C), axis=-1)
    incl = jnp.tril(jnp.ones((C, C), dtype=bool))
    strict = jnp.tril(jnp.ones((C, C), dtype=bool), -1)
    decay = jnp.exp(jnp.where(incl, G[..., :, None] - G[..., None, :], NEG_INF))
    kb = k * beta[..., None]
    m_low = jnp.where(strict, jnp.einsum('bhnid,bhnjd->bhnij', kb, k) * decay, 0.0)
    tmat = m_low + jnp.eye(C, dtype=jnp.float32)
    rhs = jnp.concatenate([v * beta[..., None], kb * jnp.exp(G)[..., None]], axis=-1)
    sol = lax.linalg.triangular_solve(tmat, rhs, left_side=True, lower=True, unit_diagonal=True)
    u, w = sol[..., :DV], sol[..., DV:]
    attn = jnp.einsum('bhnid,bhnjd->bhnij', q, k) * decay
    qg = q * jnp.exp(G)[..., None]
    kdec = k * jnp.exp(G[..., -1:] - G)[..., None]
    glast = jnp.exp(G[..., -1])

    def step(state, inp):
        u_c, w_c, attn_c, qg_c, kdec_c, gl_c = inp
        v_new = u_c - jnp.einsum('bhcd,bhde->bhce', w_c, state)
        o_c = jnp.einsum('bhcd,bhde->bhce', qg_c, state) + jnp.einsum('bhij,bhje->bhie', attn_c, v_new)
        state = state * gl_c[..., None, None] + jnp.einsum('bhcd,bhce->bhde', kdec_c, v_new)
        return state, o_c

    xs = tuple(jnp.moveaxis(t, 2, 0) for t in (u, w, attn, qg, kdec, glast))
    _, o = lax.scan(step, jnp.zeros((B, H, DK, DV), jnp.float32), xs)
    return jnp.moveaxis(o, 0, 2).reshape(B, H, S, DV)


def hybrid_mixer(h, positions, w_in, conv_w, a_log, dt_bias, dn_norm_w, w_branch_att, w_branch_dn, w_out):
    B, S, _ = h.shape
    proj = h @ w_in
    split_idx = [int(i) for i in np.cumsum(IN_SIZES)[:-1]]
    q_a, k_a, v_a, dn_qkv, z, beta_raw, alpha_raw, gate_att, gate_dn = jnp.split(proj, split_idx, axis=-1)

    def att_heads(t):
        return t.reshape(B, S, ATT_HEADS, ATT_HEAD_DIM)
    q_a = rotary(att_heads(q_a), positions)
    k_a = rotary(att_heads(k_a), positions)
    v_a = att_heads(v_a)
    outs, lses = [], []
    for gi, (window, dilation) in enumerate(ATT_GROUPS):
        sl = slice(gi * ATT_HEADS_PER_GROUP, (gi + 1) * ATT_HEADS_PER_GROUP)
        o_g, lse_g = dilated_window_attention(q_a[:, :, sl], k_a[:, :, sl], v_a[:, :, sl], window, dilation)
        outs.append(o_g)
        lses.append(lse_g)
    mix = jax.nn.softmax(jnp.stack(lses, axis=0), axis=0)
    o_att = jnp.sum(mix[..., None] * jnp.stack(outs, axis=0), axis=0)
    o_att = o_att.reshape(B, S, ATT_OUT_WIDTH).astype(h.dtype)

    dn = jax.nn.silu(short_conv(dn_qkv, conv_w))
    q_d, k_d, v_d = jnp.split(dn, 3, axis=-1)

    def dn_heads(t):
        return t.reshape(B, S, DN_HEADS, DN_HEAD_DIM).transpose(0, 2, 1, 3).astype(jnp.float32)
    q_d = l2norm(dn_heads(q_d)) * (DN_HEAD_DIM ** -0.5)
    k_d = l2norm(dn_heads(k_d))
    v_d = dn_heads(v_d)
    beta = jax.nn.sigmoid(beta_raw.astype(jnp.float32)).reshape(B, S, 2, DN_HEADS).transpose(2, 0, 3, 1)
    alpha = alpha_raw.astype(jnp.float32).reshape(B, S, 2, DN_HEADS).transpose(2, 0, 3, 1)
    log_decay = -jnp.exp(a_log.astype(jnp.float32))[:, None, :, None] * jax.nn.softplus(
        alpha + dt_bias.astype(jnp.float32)[:, None, :, None])
    o_fwd = gated_delta_chunked(q_d, k_d, v_d, beta[0], log_decay[0])
    rev = lambda t: jnp.flip(t, axis=2)
    o_bwd = rev(gated_delta_chunked(rev(q_d), rev(k_d), rev(v_d), rev(beta[1]), rev(log_decay[1])))
    o_dn = (o_fwd + o_bwd).transpose(0, 2, 1, 3)
    o_dn = o_dn * lax.rsqrt(jnp.mean(o_dn * o_dn, axis=-1, keepdims=True) + EPS) * dn_norm_w.astype(jnp.float32)
    o_dn = o_dn * jax.nn.silu(z.astype(jnp.float32).reshape(B, S, DN_HEADS, DN_HEAD_DIM))
    o_dn = o_dn.reshape(B, S, DN_WIDTH).astype(h.dtype)

    y = jax.nn.sigmoid(gate_att) * (o_att @ w_branch_att) + jax.nn.sigmoid(gate_dn) * (o_dn @ w_branch_dn)
    return y @ w_out


def setup_inputs(seed: int = 0) -> dict:
    key = jax.random.key(seed)
    ks = jax.random.split(key, 20)
    f32 = jnp.float32
    D = D_MODEL
    x = jax.random.normal(ks[0], (BATCH, SEQ, D), f32)
    c = jax.random.normal(ks[1], (BATCH, D), f32)
    offset = jax.random.randint(ks[2], (BATCH, 1), 0, 4096, dtype=jnp.int32)
    positions = (offset + jnp.arange(SEQ, dtype=jnp.int32)[None, :]).astype(jnp.int32)
    w_ada = jax.random.normal(ks[3], (DEPTH, D, 3 * N_SUBLAYERS * D), f32) * (0.5 * D ** -0.5)
    b_ada = jax.random.normal(ks[4], (DEPTH, 3 * N_SUBLAYERS * D), f32) * 0.02
    norm_pre = 1.0 + 0.02 * jax.random.normal(ks[5], (DEPTH, N_SUBLAYERS, D), f32)
    norm_post = 1.0 + 0.02 * jax.random.normal(ks[6], (DEPTH, N_SUBLAYERS, D), f32)
    ffn_w_in = jax.random.normal(ks[7], (DEPTH, 2, D, 2 * FFN_DIM), f32) * D ** -0.5
    ffn_w_out = jax.random.normal(ks[8], (DEPTH, 2, FFN_DIM, D), f32) * FFN_DIM ** -0.5
    w_in = jax.random.normal(ks[9], (DEPTH, D, N_IN), f32) * D ** -0.5
    conv_w = jax.random.normal(ks[10], (DEPTH, DN_CONV, 3 * DN_WIDTH), f32) * DN_CONV ** -0.5
    a_log = jnp.log(jax.random.uniform(ks[11], (DEPTH, 2, DN_HEADS), f32, 1.0, 16.0))
    dt = jnp.exp(jax.random.uniform(ks[12], (DEPTH, 2, DN_HEADS), f32, math.log(1e-3), math.log(1e-1)))
    dt_bias = dt + jnp.log(-jnp.expm1(-dt))
    dn_norm_w = 1.0 + 0.02 * jax.random.normal(ks[13], (DEPTH, DN_HEAD_DIM), f32)
    w_branch_att = jax.random.normal(ks[14], (DEPTH, ATT_OUT_WIDTH, D), f32) * ATT_OUT_WIDTH ** -0.5
    w_branch_dn = jax.random.normal(ks[15], (DEPTH, DN_WIDTH, D), f32) * DN_WIDTH ** -0.5
    w_out = jax.random.normal(ks[16], (DEPTH, D, D), f32) * D ** -0.5
    return {'x': x, 'c': c, 'positions': positions, 'w_ada': w_ada, 'b_ada': b_ada,
            'norm_pre': norm_pre, 'norm_post': norm_post, 'ffn_w_in': ffn_w_in, 'ffn_w_out': ffn_w_out,
            'w_in': w_in, 'conv_w': conv_w, 'a_log': a_log, 'dt_bias': dt_bias, 'dn_norm_w': dn_norm_w,
            'w_branch_att': w_branch_att, 'w_branch_dn': w_branch_dn, 'w_out': w_out}


def reference(x, c, positions, w_ada, b_ada, norm_pre, norm_post, ffn_w_in, ffn_w_out, w_in, conv_w,
              a_log, dt_bias, dn_norm_w, w_branch_att, w_branch_dn, w_out):
    B = x.shape[0]
    for l in range(DEPTH):
        mod = (jax.nn.silu(c) @ w_ada[l] + b_ada[l]).reshape(B, N_SUBLAYERS, 3, D_MODEL)
        shift, scale, gate = mod[:, :, 0], mod[:, :, 1], mod[:, :, 2]
        h = ada_pre(x, norm_pre[l, 0], shift[:, 0], scale[:, 0])
        y = swiglu(h, ffn_w_in[l, 0], ffn_w_out[l, 0])
        x = x + 0.5 * gate[:, 0, None, :] * rmsnorm(y, norm_post[l, 0])
        h = ada_pre(x, norm_pre[l, 1], shift[:, 1], scale[:, 1])
        y = hybrid_mixer(h, positions, w_in[l], conv_w[l], a_log[l], dt_bias[l], dn_norm_w[l],
                         w_branch_att[l], w_branch_dn[l], w_out[l])
        x = x + gate[:, 1, None, :] * rmsnorm(y, norm_post[l, 1])
        h = ada_pre(x, norm_pre[l, 2], shift[:, 2], scale[:, 2])
        y = swiglu(h, ffn_w_in[l, 1], ffn_w_out[l, 1])
        x = x + 0.5 * gate[:, 2, None, :] * rmsnorm(y, norm_post[l, 2])
    return x
```

```python
import functools
import math

import jax
import jax.numpy as jnp
from jax import lax
from jax.experimental import pallas as pl
from jax.experimental.pallas import tpu as pltpu

D_MODEL = 1024
DEPTH = 2
N_SUBLAYERS = 3
ATT_GROUPS = ((128, 1), (512, 4), (2048, 16))
ATT_HEADS_PER_GROUP = 4
ATT_HEAD_DIM = 64
ATT_WIDTH = len(ATT_GROUPS) * ATT_HEADS_PER_GROUP * ATT_HEAD_DIM
ATT_GROUP_WIDTH = ATT_HEADS_PER_GROUP * ATT_HEAD_DIM
ATT_RADIUS = 64
ROPE_THETA = 10000.0
DN_HEADS = 6
DN_HEAD_DIM = 128
DN_WIDTH = DN_HEADS * DN_HEAD_DIM
DN_CONV = 5
FFN_DIM = 2816
EPS = 1e-6
NEG_INF = -1e30

LANES = 128
SUBLANES = 8
BF16_ROWS = 16
VMEM_LIMIT_BYTES = 56 * 1024 * 1024

ROW_TILE = 512
FFN_CHUNKS = 2
ATT_Q_TILE = 1024
ATT_Q_SUB = 128
DN_CHUNK = 128
DN_BASE = 16
HALO = BF16_ROWS

_BF = jnp.bfloat16
_F32 = jnp.float32


def _cparams(*sem):
    return pltpu.CompilerParams(dimension_semantics=sem, vmem_limit_bytes=VMEM_LIMIT_BYTES)


def _resident(shape):
    nd = len(shape)
    return pl.BlockSpec(shape, lambda *_: (0,) * nd, pipeline_mode=pl.Buffered(1))


def _rms(t):
    return t * lax.rsqrt(jnp.mean(t * t, axis=-1, keepdims=True) + EPS)


def _dot(a, b):
    return jnp.dot(a, b, preferred_element_type=_F32)


def _dot_nt(a, b):
    return lax.dot_general(a, b, (((1,), (1,)), ((), ())), preferred_element_type=_F32)


def _dot_tn(a, b):
    return lax.dot_general(a, b, (((0,), (0,)), ((), ())), preferred_element_type=_F32)


def _ada_kernel(c_ref, w_ref, b_ref, o_ref):
    c = c_ref[...]
    s = c * jax.nn.sigmoid(c)
    o_ref[0] = jnp.sum(w_ref[0] * s, axis=0, keepdims=True) + b_ref[0]


def _ada_modulation(c, w_ada, b_ada):
    depth, d, n = w_ada.shape
    tn = 1024
    out = pl.pallas_call(
        _ada_kernel,
        grid=(depth, n // tn),
        in_specs=[
            pl.BlockSpec((d, 1), lambda l, j: (0, 0)),
            pl.BlockSpec((1, d, tn), lambda l, j: (l, 0, j)),
            pl.BlockSpec((1, 1, tn), lambda l, j: (l, 0, j)),
        ],
        out_specs=pl.BlockSpec((1, 1, tn), lambda l, j: (l, 0, j)),
        out_shape=jax.ShapeDtypeStruct((depth, 1, n), _F32),
        compiler_params=_cparams("arbitrary", "arbitrary"),
    )(c.reshape(d, 1), w_ada, b_ada.reshape(depth, 1, n))
    return out.reshape(depth, N_SUBLAYERS, 3, d)


def _rope_kernel(pos_ref, inv_ref, cos_ref, sin_ref):
    ang = pos_ref[...].astype(_F32) * inv_ref[...]
    lane = lax.broadcasted_iota(jnp.int32, ang.shape, 1)
    first_half = (lane % ATT_HEAD_DIM) < (ATT_HEAD_DIM // 2)
    cos_ref[...] = jnp.cos(ang)
    sin_ref[...] = jnp.where(first_half, -jnp.sin(ang), jnp.sin(ang))


def _rope_tables(positions):
    s = positions.shape[1]
    ts = min(2048, s)
    half = ATT_HEAD_DIM // 2
    inv = ROPE_THETA ** (-jnp.arange(half, dtype=_F32) * 2.0 / ATT_HEAD_DIM)
    inv_row = jnp.tile(inv, LANES // half).reshape(1, LANES)
    return pl.pallas_call(
        _rope_kernel,
        grid=(s // ts,),
        in_specs=[pl.BlockSpec((ts, 1), lambda i: (i, 0)), pl.BlockSpec((1, LANES), lambda i: (0, 0))],
        out_specs=[pl.BlockSpec((ts, LANES), lambda i: (i, 0))] * 2,
        out_shape=[jax.ShapeDtypeStruct((s, LANES), _F32)] * 2,
        compiler_params=_cparams("arbitrary"),
    )(positions.reshape(s, 1), inv_row)


def _ada_pre(x, vec_ref):
    a = vec_ref[0:1, :] * (1.0 + vec_ref[2:3, :])
    return _rms(x) * a + vec_ref[1:2, :]


def _ffn_kernel(x_ref, vec_ref, win_ref, wout_ref, o_ref, act_ref):
    x = x_ref[...]
    hb = _ada_pre(x, vec_ref).astype(_BF)
    fc = FFN_DIM // FFN_CHUNKS
    for j in range(FFN_CHUNKS):
        g = _dot(hb, win_ref[:, j * fc:(j + 1) * fc])
        u = _dot(hb, win_ref[:, FFN_DIM + j * fc:FFN_DIM + (j + 1) * fc])
        act_ref[:, j * fc:(j + 1) * fc] = (g * jax.nn.sigmoid(g) * u).astype(_BF)
    y = _dot(act_ref[...], wout_ref[...])
    o_ref[...] = x + (0.5 * vec_ref[3:4, :]) * (_rms(y) * vec_ref[4:5, :])


def _ffn_sublayer(x, vec, w_in, w_out):
    s, d = x.shape
    tm = min(ROW_TILE, s)
    return pl.pallas_call(
        _ffn_kernel,
        grid=(s // tm,),
        in_specs=[
            pl.BlockSpec((tm, d), lambda i: (i, 0)),
            _resident(vec.shape),
            _resident(w_in.shape),
            _resident(w_out.shape),
        ],
        out_specs=pl.BlockSpec((tm, d), lambda i: (i, 0)),
        out_shape=jax.ShapeDtypeStruct((s, d), _F32),
        scratch_shapes=[pltpu.VMEM((tm, FFN_DIM), _BF)],
        compiler_params=_cparams("arbitrary"),
    )(x, vec, w_in, w_out)


def _swap_halves(t):
    n = t.shape[1]
    half = ATT_HEAD_DIM // 2
    from_right = pltpu.roll(t, n - half, 1)
    from_left = pltpu.roll(t, half, 1)
    lane = lax.broadcasted_iota(jnp.int32, t.shape, 1)
    return jnp.where((lane % ATT_HEAD_DIM) < half, from_right, from_left)


def _chunk_cumsums(g, tm):
    row = lax.broadcasted_iota(jnp.int32, g.shape, 0) % DN_CHUNK
    fwd, bwd = g, g
    sh = 1
    while sh < DN_CHUNK:
        fwd = fwd + jnp.where(row >= sh, pltpu.roll(fwd, sh, 0), 0.0)
        bwd = bwd + jnp.where(row < DN_CHUNK - sh, pltpu.roll(bwd, tm - sh, 0), 0.0)
        sh *= 2
    return fwd, bwd


def _proj_kernel(xp_ref, x_ref, xn_ref, vec_ref, cos_ref, sin_ref, watt_ref, wdn_ref, wz_ref, wg_ref, wba_ref,
                 conv_ref, dec_ref,
                 q_ref, k_ref, v_ref, qd_ref, kd_ref, vd_ref, z_ref, gate_ref, tab_ref, tabt_ref,
                 pdn_ref, *, tm):
    i = pl.program_id(0)
    last = pl.num_programs(0) - 1
    x_ext = jnp.concatenate([xp_ref[0], x_ref[...], xn_ref[0]], axis=0)
    h_ext = _ada_pre(x_ext, vec_ref).astype(_BF)
    hb = h_ext[HALO:HALO + tm]

    pa = _dot(hb, watt_ref[...])
    reps = ATT_WIDTH // LANES
    cos = jnp.tile(cos_ref[...], (1, reps))
    sin = jnp.tile(sin_ref[...], (1, reps))
    q = pa[:, :ATT_WIDTH]
    k = pa[:, ATT_WIDTH:2 * ATT_WIDTH]
    q_ref[...] = ((q * cos + _swap_halves(q) * sin) * (ATT_HEAD_DIM ** -0.5)).astype(_BF)
    k_ref[...] = (k * cos + _swap_halves(k) * sin).astype(_BF)
    v_ref[...] = pa[:, 2 * ATT_WIDTH:].astype(_BF)

    pdn = _dot(h_ext, wdn_ref[...])
    row = lax.broadcasted_iota(jnp.int32, (tm + 2 * HALO, 1), 0)
    outside = ((i == 0) & (row < HALO)) | ((i == last) & (row >= HALO + tm))
    pdn_ref[...] = jnp.where(outside, 0.0, pdn)
    pad = DN_CONV // 2
    acc = pdn_ref[HALO - pad:HALO - pad + tm, :] * conv_ref[0:1, :]
    for t in range(1, DN_CONV):
        acc = acc + pdn_ref[HALO - pad + t:HALO - pad + t + tm, :] * conv_ref[t:t + 1, :]
    dn = acc * jax.nn.sigmoid(acc)
    for hd in range(DN_HEADS):
        sl = slice(hd * DN_HEAD_DIM, (hd + 1) * DN_HEAD_DIM)
        qh = dn[:, sl]
        kh = dn[:, DN_WIDTH + hd * DN_HEAD_DIM:DN_WIDTH + (hd + 1) * DN_HEAD_DIM]
        qn = qh * lax.rsqrt(jnp.sum(qh * qh, axis=-1, keepdims=True) + EPS) * (DN_HEAD_DIM ** -0.5)
        kn = kh * lax.rsqrt(jnp.sum(kh * kh, axis=-1, keepdims=True) + EPS)
        qd_ref[:, sl] = qn.astype(_BF)
        kd_ref[:, sl] = kn.astype(_BF)
    vd_ref[...] = dn[:, 2 * DN_WIDTH:].astype(_BF)

    z = _dot(hb, wz_ref[...])
    z_ref[...] = (z * jax.nn.sigmoid(z)).astype(_BF)
    gate_ref[...] = jax.nn.sigmoid(_dot(hb, wg_ref[...])).astype(_BF)

    raw = _dot(hb, wba_ref[...])
    lane = lax.broadcasted_iota(jnp.int32, raw.shape, 1)
    beta = jax.nn.sigmoid(raw)
    log_decay = dec_ref[0:1, :] * jax.nn.softplus(raw + dec_ref[1:2, :])
    cum_f, cum_b = _chunk_cumsums(log_decay, tm)
    nh = DN_HEADS
    tab = jnp.where(lane < 2 * nh, beta, jnp.where(lane < 3 * nh, cum_f, jnp.where(lane < 4 * nh, cum_b, 0.0)))
    tab_ref[...] = tab
    tabt_ref[...] = tab.T[:4 * nh, :]


def _mixer_projection(x, vec, cos_t, sin_t, wts, conv_w, dec):
    s, d = x.shape
    tm = min(ROW_TILE, s)
    nt = s // tm
    hb = tm // HALO
    x3 = x.reshape(s // HALO, HALO, d)
    row = lambda i: (i, 0)
    w_att, w_dn, w_z, w_g, w_ba = wts
    bf = lambda n: jax.ShapeDtypeStruct((s, n), _BF)
    outs = pl.pallas_call(
        functools.partial(_proj_kernel, tm=tm),
        grid=(nt,),
        in_specs=[
            pl.BlockSpec((1, HALO, d), lambda i: (jnp.maximum(i * hb - 1, 0), 0, 0)),
            pl.BlockSpec((tm, d), row),
            pl.BlockSpec((1, HALO, d), lambda i: (jnp.minimum((i + 1) * hb, s // HALO - 1), 0, 0)),
            _resident(vec.shape),
            pl.BlockSpec((tm, LANES), row),
            pl.BlockSpec((tm, LANES), row),
            _resident(w_att.shape), _resident(w_dn.shape), _resident(w_z.shape), _resident(w_g.shape),
            _resident(w_ba.shape), _resident(conv_w.shape), _resident(dec.shape),
        ],
        out_specs=[
            pl.BlockSpec((tm, ATT_WIDTH), row), pl.BlockSpec((tm, ATT_WIDTH), row), pl.BlockSpec((tm, ATT_WIDTH), row),
            pl.BlockSpec((tm, DN_WIDTH), row), pl.BlockSpec((tm, DN_WIDTH), row), pl.BlockSpec((tm, DN_WIDTH), row),
            pl.BlockSpec((tm, DN_WIDTH), row), pl.BlockSpec((tm, 2 * d), row),
            pl.BlockSpec((tm, LANES), row), pl.BlockSpec((4 * DN_HEADS, tm), lambda i: (0, i)),
        ],
        out_shape=[bf(ATT_WIDTH), bf(ATT_WIDTH), bf(ATT_WIDTH), bf(DN_WIDTH), bf(DN_WIDTH), bf(DN_WIDTH),
                   bf(DN_WIDTH), bf(2 * d),
                   jax.ShapeDtypeStruct((s, LANES), _F32), jax.ShapeDtypeStruct((4 * DN_HEADS, s), _F32)],
        scratch_shapes=[pltpu.VMEM((tm + 2 * HALO, 3 * DN_WIDTH), _F32)],
        compiler_params=_cparams("arbitrary"),
    )(x3, x, x3, vec, cos_t, sin_t, w_att, w_dn, w_z, w_g, w_ba, conv_w, dec)
    return outs


def _att_kernel(q_ref, kp_ref, k_ref, kn_ref, vp_ref, v_ref, vn_ref, o_ref, lse_ref, kbuf, vbuf, *, nq, n_rows):
    i = pl.program_id(1)
    r = ATT_RADIUS
    kbuf[0:r, :] = kp_ref[...]
    kbuf[r:r + nq, :] = k_ref[...]
    kbuf[r + nq:, :] = kn_ref[...]
    vbuf[0:r, :] = vp_ref[...]
    vbuf[r:r + nq, :] = v_ref[...]
    vbuf[r + nq:, :] = vn_ref[...]

    qs = ATT_Q_SUB
    kw = qs + 2 * r
    qi = lax.broadcasted_iota(jnp.int32, (qs, kw), 0)
    kj = lax.broadcasted_iota(jnp.int32, (qs, kw), 1) - r
    band = jnp.where(jnp.abs(kj - qi) <= r, 0.0, NEG_INF)
    lane = lax.broadcasted_iota(jnp.int32, (1, LANES), 1)
    low = lane < ATT_HEAD_DIM
    kcol = lax.broadcasted_iota(jnp.int32, (1, kw), 1)
    for j in range(nq // qs):
        r0 = j * qs
        kidx = i * nq + r0 - r + kcol
        edge = jnp.where((kidx < 0) | (kidx >= n_rows), NEG_INF, 0.0)
        bias = band + edge
        bias2 = jnp.concatenate([bias, bias], axis=0)
        for hp in range(ATT_GROUP_WIDTH // LANES):
            cs = slice(hp * LANES, (hp + 1) * LANES)
            qp = q_ref[r0:r0 + qs, cs]
            kp = kbuf[r0:r0 + kw, cs]
            vp = vbuf[r0:r0 + kw, cs]
            zero = jnp.zeros_like(qp)
            qq = jnp.concatenate([jnp.where(low, qp, zero), jnp.where(low, zero, qp)], axis=0)
            sc = _dot_nt(qq, kp) + bias2
            m = jnp.max(sc, axis=-1, keepdims=True)
            p = jnp.exp(sc - m)
            den = jnp.sum(p, axis=-1, keepdims=True)
            pv = _dot(p.astype(_BF), vp) / den
            lse = m + jnp.log(den)
            o_ref[r0:r0 + qs, cs] = jnp.where(low, pv[:qs], pv[qs:]).astype(_BF)
            lse_ref[r0:r0 + qs, cs] = jnp.where(low, lse[:qs], lse[qs:])


def _dilated_attention(q, k, v, group, dilation):
    s = q.shape[0]
    n_rows = s // dilation
    width = dilation * ATT_WIDTH
    qv, kv, vv = (t.reshape(n_rows, width) for t in (q, k, v))
    nq = min(ATT_Q_TILE, n_rows)
    nt = n_rows // nq
    r = ATT_RADIUS
    gw = ATT_GROUP_WIDTH
    ngroups = ATT_WIDTH // gw
    col = lambda c: c * ngroups + group
    main = pl.BlockSpec((nq, gw), lambda c, i: (i, col(c)))
    prev = pl.BlockSpec((r, gw), lambda c, i: (jnp.maximum(i * (nq // r) - 1, 0), col(c)))
    nxt = pl.BlockSpec((r, gw), lambda c, i: (jnp.minimum((i + 1) * (nq // r), n_rows // r - 1), col(c)))
    out = pl.BlockSpec((nq, gw), lambda c, i: (i, c))
    o, lse = pl.pallas_call(
        functools.partial(_att_kernel, nq=nq, n_rows=n_rows),
        grid=(dilation, nt),
        in_specs=[main, prev, main, nxt, prev, main, nxt],
        out_specs=[out, out],
        out_shape=[jax.ShapeDtypeStruct((n_rows, dilation * gw), _BF),
                   jax.ShapeDtypeStruct((n_rows, dilation * gw), _F32)],
        scratch_shapes=[pltpu.VMEM((nq + 2 * r, gw), _BF), pltpu.VMEM((nq + 2 * r, gw), _BF)],
        compiler_params=_cparams("arbitrary", "arbitrary"),
    )(qv, kv, kv, kv, vv, vv, vv)
    return o.reshape(s, gw), lse.reshape(s, gw)


def _unit_triangular_inverse(m, ri, ci, upper):
    c = m.shape[0]
    b = DN_BASE
    nb = c // b
    lane = lax.broadcasted_iota(jnp.int32, (b, c), 1)
    sub = lax.broadcasted_iota(jnp.int32, (b, c), 0)
    blk = lane // b
    dg = jnp.zeros((b, c), _F32)
    for bi in range(nb):
        dg = jnp.where(blk == bi, m[bi * b:(bi + 1) * b, :], dg)
    acc = jnp.where(sub == lane % b, 1.0, 0.0)
    order = range(b - 1, 0, -1) if upper else range(b - 1)
    for j in order:
        cj = jnp.take_along_axis(dg, blk * b + j, axis=1)
        acc = acc - cj * acc[j:j + 1, :]
    x = jnp.concatenate([jnp.where(blk == bi, acc, 0.0) for bi in range(nb)], axis=0)
    size = b
    while size < c:
        lo = jnp.where((ri // (2 * size) == ci // (2 * size)) & (ri // size != ci // size), m, 0.0)
        xb = x.astype(_BF)
        x = x - _dot(_dot(xb, lo.astype(_BF)).astype(_BF), xb)
        size *= 2
    return x


def _delta_chunk(q, k, v, kk, qk, beta, gcol, grow, glast, state, upper):
    c = DN_CHUNK
    ri = lax.broadcasted_iota(jnp.int32, (c, c), 0)
    ci = lax.broadcasted_iota(jnp.int32, (c, c), 1)
    incl = (ri <= ci) if upper else (ri >= ci)
    strict = (ri < ci) if upper else (ri > ci)
    decay = jnp.exp(jnp.where(incl, gcol - grow, NEG_INF))
    m = jnp.where(strict, kk * beta * decay, 0.0)
    attn = qk * decay
    x = _unit_triangular_inverse(m, ri, ci, upper)
    eg = jnp.exp(gcol)
    rhs = jnp.concatenate([v * beta, k * (beta * eg)], axis=1).astype(_BF)
    sol = _dot(x.astype(_BF), rhs)
    u, w = sol[:, :DN_HEAD_DIM], sol[:, DN_HEAD_DIM:]
    sb = state.astype(_BF)
    ws = _dot(jnp.concatenate([w, q * eg], axis=0).astype(_BF), sb)
    v_new = u - ws[:c]
    vb = v_new.astype(_BF)
    o = ws[c:] + _dot(attn.astype(_BF), vb)
    kdec = (k * jnp.exp(glast - gcol)).astype(_BF)
    new_state = state * jnp.exp(glast) + _dot_tn(kdec, vb)
    return o, new_state


def _dn_kernel(qf_ref, kf_ref, vf_ref, tf_ref, ttf_ref, qb_ref, kb_ref, vb_ref, tb_ref, ttb_ref,
               of_ref, ob_ref, state_ref):
    @pl.when(pl.program_id(0) == 0)
    def _():
        state_ref[...] = jnp.zeros_like(state_ref)

    c = DN_CHUNK
    nh = DN_HEADS
    for direction, (q_ref, k_ref, v_ref, t_ref, tt_ref, o_ref) in enumerate(
            ((qf_ref, kf_ref, vf_ref, tf_ref, ttf_ref, of_ref), (qb_ref, kb_ref, vb_ref, tb_ref, ttb_ref, ob_ref))):
        upper = direction == 1
        tab = t_ref[...]
        tabt = tt_ref[...]
        for hd in range(nh):
            sl = slice(hd * DN_HEAD_DIM, (hd + 1) * DN_HEAD_DIM)
            qb, kb = q_ref[:, sl], k_ref[:, sl]
            kk = _dot_nt(kb, kb)
            qk = _dot_nt(qb, kb)
            lb = direction * nh + hd
            lg = 2 * nh + direction * nh + hd
            beta = tab[:, lb:lb + 1]
            gcol = tab[:, lg:lg + 1]
            grow = tabt[lg:lg + 1, :]
            glast = gcol[0:1, :] if upper else gcol[c - 1:c, :]
            st = direction * nh + hd
            o, new_state = _delta_chunk(qb.astype(_F32), kb.astype(_F32), v_ref[:, sl].astype(_F32), kk, qk,
                                        beta, gcol, grow, glast, state_ref[st], upper)
            o_ref[:, sl] = o
            state_ref[st] = new_state


def _gated_delta(qd, kd, vd, tab, tabt):
    s = qd.shape[0]
    c = DN_CHUNK
    n = s // c
    fwd = lambda i: (i, 0)
    bwd = lambda i: (n - 1 - i, 0)
    wide = lambda im: pl.BlockSpec((c, DN_WIDTH), im)
    tspec = lambda im: pl.BlockSpec((c, LANES), im)
    ttspec = lambda flip: pl.BlockSpec((4 * DN_HEADS, c), (lambda i: (0, n - 1 - i)) if flip else (lambda i: (0, i)))
    return pl.pallas_call(
        _dn_kernel,
        grid=(n,),
        in_specs=[wide(fwd), wide(fwd), wide(fwd), tspec(fwd), ttspec(False),
                  wide(bwd), wide(bwd), wide(bwd), tspec(bwd), ttspec(True)],
        out_specs=[wide(fwd), wide(bwd)],
        out_shape=[jax.ShapeDtypeStruct((s, DN_WIDTH), _F32)] * 2,
        scratch_shapes=[pltpu.VMEM((2 * DN_HEADS, DN_HEAD_DIM, DN_HEAD_DIM), _F32)],
        compiler_params=_cparams("arbitrary"),
    )(qd, kd, vd, tab, tabt, qd, kd, vd, tab, tabt)


def _out_kernel(x_ref, o0_ref, o1_ref, o2_ref, l0_ref, l1_ref, l2_ref, of_ref, ob_ref, z_ref, gate_ref, vec_ref,
                wa_ref, wd_ref, wo_ref, out_ref, odn_ref):
    d = D_MODEL
    l0, l1, l2 = l0_ref[...], l1_ref[...], l2_ref[...]
    mx = jnp.maximum(jnp.maximum(l0, l1), l2)
    e0, e1, e2 = jnp.exp(l0 - mx), jnp.exp(l1 - mx), jnp.exp(l2 - mx)
    o_att = (e0 * o0_ref[...].astype(_F32) + e1 * o1_ref[...].astype(_F32) + e2 * o2_ref[...].astype(_F32)) / (
        e0 + e1 + e2)
    for hd in range(DN_HEADS):
        sl = slice(hd * DN_HEAD_DIM, (hd + 1) * DN_HEAD_DIM)
        o = of_ref[:, sl] + ob_ref[:, sl]
        o = _rms(o) * vec_ref[5:6, 0:DN_HEAD_DIM]
        odn_ref[:, sl] = (o * z_ref[:, sl].astype(_F32)).astype(_BF)
    ya = _dot(o_att.astype(_BF), wa_ref[...])
    yd = _dot(odn_ref[...], wd_ref[...])
    y = gate_ref[:, :d].astype(_F32) * ya + gate_ref[:, d:].astype(_F32) * yd
    y = _dot(y.astype(_BF), wo_ref[...])
    out_ref[...] = x_ref[...] + vec_ref[3:4, :] * (_rms(y) * vec_ref[4:5, :])


def _mixer_output(x, att, o_f, o_b, z, gates, vec, w_a, w_d, w_o):
    s, d = x.shape
    tm = min(ROW_TILE, s)
    row = lambda i: (i, 0)
    (o0, l0), (o1, l1), (o2, l2) = att
    gw = ATT_GROUP_WIDTH
    return pl.pallas_call(
        _out_kernel,
        grid=(s // tm,),
        in_specs=[pl.BlockSpec((tm, d), row)] + [pl.BlockSpec((tm, gw), row)] * 6
        + [pl.BlockSpec((tm, DN_WIDTH), row)] * 3 + [pl.BlockSpec((tm, 2 * d), row)]
        + [_resident(vec.shape), _resident(w_a.shape), _resident(w_d.shape), _resident(w_o.shape)],
        out_specs=pl.BlockSpec((tm, d), row),
        out_shape=jax.ShapeDtypeStruct((s, d), _F32),
        scratch_shapes=[pltpu.VMEM((tm, DN_WIDTH), _BF)],
        compiler_params=_cparams("arbitrary"),
    )(x, o0, o1, o2, l0, l1, l2, o_f, o_b, z, gates, vec, w_a, w_d, w_o)


def _vec(rows, d):
    pad = [jnp.zeros((d,), _F32)] * (SUBLANES - len(rows))
    return jnp.stack(list(rows) + pad, axis=0)


def kernel(x, c, positions, w_ada, b_ada, norm_pre, norm_post, ffn_w_in, ffn_w_out, w_in, conv_w, a_log, dt_bias,
           dn_norm_w, w_branch_att, w_branch_dn, w_out):
    b, s, d = x.shape
    assert b == 1 and d == D_MODEL and s % (DN_CHUNK * 16) == 0
    depth = w_ada.shape[0]
    x2 = x.reshape(s, d)
    mod = _ada_modulation(c, w_ada, b_ada)
    cos_t, sin_t = _rope_tables(positions)

    o_dn = 3 * ATT_WIDTH
    o_z = o_dn + 3 * DN_WIDTH
    o_ba = o_z + DN_WIDTH
    o_g = o_ba + 4 * DN_HEADS
    nh = DN_HEADS
    zeros_row = jnp.zeros((LANES,), _F32)

    for l in range(depth):
        def vec_for(sub, extra=None):
            rows = [norm_pre[l, sub], mod[l, sub, 0], mod[l, sub, 1], mod[l, sub, 2], norm_post[l, sub]]
            if extra is not None:
                rows.append(extra)
            return _vec(rows, d)

        x2 = _ffn_sublayer(x2, vec_for(0), ffn_w_in[l, 0].astype(_BF), ffn_w_out[l, 0].astype(_BF))

        wl = w_in[l]
        w_ba = jnp.pad(wl[:, o_ba:o_g], ((0, 0), (0, LANES - 4 * nh))).astype(_BF)
        wts = (wl[:, :o_dn].astype(_BF), wl[:, o_dn:o_z].astype(_BF), wl[:, o_z:o_ba].astype(_BF),
               wl[:, o_g:].astype(_BF), w_ba)
        neg_a = zeros_row.at[2 * nh:4 * nh].set(-jnp.exp(a_log[l].astype(_F32)).reshape(-1))
        dtb = zeros_row.at[2 * nh:4 * nh].set(dt_bias[l].astype(_F32).reshape(-1))
        dec = _vec([neg_a, dtb], LANES)
        qa, ka, va, qd, kd, vd, z, gates, tab, tabt = _mixer_projection(
            x2, vec_for(1), cos_t, sin_t, wts, conv_w[l], dec)
        att = [_dilated_attention(qa, ka, va, g, dil) for g, (_, dil) in enumerate(ATT_GROUPS)]
        o_f, o_b = _gated_delta(qd, kd, vd, tab, tabt)
        x2 = _mixer_output(x2, att, o_f, o_b, z, gates, vec_for(1, jnp.tile(dn_norm_w[l], d // DN_HEAD_DIM)),
                           w_branch_att[l].astype(_BF), w_branch_dn[l].astype(_BF), w_out[l].astype(_BF))

        x2 = _ffn_sublayer(x2, vec_for(2), ffn_w_in[l, 1].astype(_BF), ffn_w_out[l, 1].astype(_BF))
    return x2.reshape(b, s, d)
```

```python
import functools
import math

import jax
import jax.numpy as jnp
from jax import lax
from jax.experimental import pallas as pl
from jax.experimental.pallas import tpu as pltpu

D_MODEL = 1024
DEPTH = 2
N_SUBLAYERS = 3
ATT_GROUPS = ((128, 1), (512, 4), (2048, 16))
ATT_HEADS_PER_GROUP = 4
ATT_HEAD_DIM = 64
ATT_WIDTH = len(ATT_GROUPS) * ATT_HEADS_PER_GROUP * ATT_HEAD_DIM
ATT_GROUP_WIDTH = ATT_HEADS_PER_GROUP * ATT_HEAD_DIM
ATT_RADIUS = 64
ROPE_THETA = 10000.0
DN_HEADS = 6
DN_HEAD_DIM = 128
DN_WIDTH = DN_HEADS * DN_HEAD_DIM
DN_CONV = 5
FFN_DIM = 2816
EPS = 1e-6
NEG_INF = -1e30

LANES = 128
SUBLANES = 8
BF16_ROWS = 16
VMEM_LIMIT_BYTES = 56 * 1024 * 1024

ROW_TILE = 512
FFN_CHUNKS = 2
ATT_Q_TILE = 1024
ATT_Q_SUB = 128
DN_CHUNK = 128
DN_BASE = 16
HALO = BF16_ROWS

_BF = jnp.bfloat16
_F32 = jnp.float32


def _cparams(*sem):
    return pltpu.CompilerParams(dimension_semantics=sem, vmem_limit_bytes=VMEM_LIMIT_BYTES)


def _resident(shape):
    nd = len(shape)
    return pl.BlockSpec(shape, lambda *_: (0,) * nd, pipeline_mode=pl.Buffered(1))


def _rms(t):
    return t * lax.rsqrt(jnp.mean(t * t, axis=-1, keepdims=True) + EPS)


def _dot(a, b):
    return jnp.dot(a, b, preferred_element_type=_F32)


def _dot_nt(a, b):
    return lax.dot_general(a, b, (((1,), (1,)), ((), ())), preferred_element_type=_F32)


def _dot_tn(a, b):
    return lax.dot_general(a, b, (((0,), (0,)), ((), ())), preferred_element_type=_F32)


def _ada_kernel(c_ref, w_ref, b_ref, o_ref):
    c = c_ref[...]
    s = c * jax.nn.sigmoid(c)
    o_ref[0] = jnp.sum(w_ref[0] * s, axis=0, keepdims=True) + b_ref[0]


def _ada_modulation(c, w_ada, b_ada):
    depth, d, n = w_ada.shape
    tn = 1024
    out = pl.pallas_call(
        _ada_kernel,
        grid=(depth, n // tn),
        in_specs=[
            pl.BlockSpec((d, 1), lambda l, j: (0, 0)),
            pl.BlockSpec((1, d, tn), lambda l, j: (l, 0, j)),
            pl.BlockSpec((1, 1, tn), lambda l, j: (l, 0, j)),
        ],
        out_specs=pl.BlockSpec((1, 1, tn), lambda l, j: (l, 0, j)),
        out_shape=jax.ShapeDtypeStruct((depth, 1, n), _F32),
        compiler_params=_cparams("arbitrary", "arbitrary"),
    )(c.reshape(d, 1), w_ada, b_ada.reshape(depth, 1, n))
    return out.reshape(depth, N_SUBLAYERS, 3, d)


def _rope_kernel(pos_ref, inv_ref, cos_ref, sin_ref):
    ang = pos_ref[...].astype(_F32) * inv_ref[...]
    lane = lax.broadcasted_iota(jnp.int32, ang.shape, 1)
    first_half = (lane % ATT_HEAD_DIM) < (ATT_HEAD_DIM // 2)
    cos_ref[...] = jnp.cos(ang)
    sin_ref[...] = jnp.where(first_half, -jnp.sin(ang), jnp.sin(ang))


def _rope_tables(positions):
    s = positions.shape[1]
    ts = min(2048, s)
    half = ATT_HEAD_DIM // 2
    inv = ROPE_THETA ** (-jnp.arange(half, dtype=_F32) * 2.0 / ATT_HEAD_DIM)
    inv_row = jnp.tile(inv, LANES // half).reshape(1, LANES)
    return pl.pallas_call(
        _rope_kernel,
        grid=(s // ts,),
        in_specs=[pl.BlockSpec((ts, 1), lambda i: (i, 0)), pl.BlockSpec((1, LANES), lambda i: (0, 0))],
        out_specs=[pl.BlockSpec((ts, LANES), lambda i: (i, 0))] * 2,
        out_shape=[jax.ShapeDtypeStruct((s, LANES), _F32)] * 2,
        compiler_params=_cparams("arbitrary"),
    )(positions.reshape(s, 1), inv_row)


def _ada_pre(x, vec_ref):
    a = vec_ref[0:1, :] * (1.0 + vec_ref[2:3, :])
    return _rms(x) * a + vec_ref[1:2, :]


def _ffn_kernel(x_ref, vec_ref, win_ref, wout_ref, o_ref, act_ref):
    x = x_ref[...]
    hb = _ada_pre(x, vec_ref).astype(_BF)
    fc = FFN_DIM // FFN_CHUNKS
    for j in range(FFN_CHUNKS):
        g = _dot(hb, win_ref[:, j * fc:(j + 1) * fc])
        u = _dot(hb, win_ref[:, FFN_DIM + j * fc:FFN_DIM + (j + 1) * fc])
        act_ref[:, j * fc:(j + 1) * fc] = (g * jax.nn.sigmoid(g) * u).astype(_BF)
    y = _dot(act_ref[...], wout_ref[...])
    o_ref[...] = x + (0.5 * vec_ref[3:4, :]) * (_rms(y) * vec_ref[4:5, :])


def _ffn_sublayer(x, vec, w_in, w_out):
    s, d = x.shape
    tm = min(ROW_TILE, s)
    return pl.pallas_call(
        _ffn_kernel,
        grid=(s // tm,),
        in_specs=[
            pl.BlockSpec((tm, d), lambda i: (i, 0)),
            _resident(vec.shape),
            _resident(w_in.shape),
            _resident(w_out.shape),
        ],
        out_specs=pl.BlockSpec((tm, d), lambda i: (i, 0)),
        out_shape=jax.ShapeDtypeStruct((s, d), _F32),
        scratch_shapes=[pltpu.VMEM((tm, FFN_DIM), _BF)],
        compiler_params=_cparams("arbitrary"),
    )(x, vec, w_in, w_out)


def _swap_halves(t):
    n = t.shape[1]
    half = ATT_HEAD_DIM // 2
    from_right = pltpu.roll(t, n - half, 1)
    from_left = pltpu.roll(t, half, 1)
    lane = lax.broadcasted_iota(jnp.int32, t.shape, 1)
    return jnp.where((lane % ATT_HEAD_DIM) < half, from_right, from_left)


def _chunk_cumsums(g, tm):
    row = lax.broadcasted_iota(jnp.int32, g.shape, 0) % DN_CHUNK
    fwd, bwd = g, g
    sh = 1
    while sh < DN_CHUNK:
        fwd = fwd + jnp.where(row >= sh, pltpu.roll(fwd, sh, 0), 0.0)
        bwd = bwd + jnp.where(row < DN_CHUNK - sh, pltpu.roll(bwd, tm - sh, 0), 0.0)
        sh *= 2
    return fwd, bwd


def _store_by_class(val, out_refs, stage_ref, tm):
    gw = ATT_GROUP_WIDTH
    for g, (_, dil) in enumerate(ATT_GROUPS):
        if dil == 1:
            out_refs[g][...] = val[:, g * gw:(g + 1) * gw].astype(_BF)
            continue
        for c in range(gw // LANES):
            stage_ref[c] = val[:, g * gw + c * LANES:g * gw + (c + 1) * LANES]
        for r in range(dil):
            for c in range(gw // LANES):
                out_refs[g][:, r * gw + c * LANES:r * gw + (c + 1) * LANES] = (
                    stage_ref[c, pl.ds(r, tm // dil, stride=dil), :].astype(_BF))


def _proj_kernel(xp_ref, x_ref, xn_ref, vec_ref, cos_ref, sin_ref, watt_ref, wdn_ref, wz_ref, wg_ref, wba_ref,
                 conv_ref, dec_ref,
                 q0_ref, q1_ref, q2_ref, k0_ref, k1_ref, k2_ref, v0_ref, v1_ref, v2_ref,
                 qd_ref, kd_ref, vd_ref, z_ref, gate_ref, tab_ref, tabt_ref,
                 pdn_ref, stage_ref, *, tm):
    i = pl.program_id(0)
    last = pl.num_programs(0) - 1
    x_ext = jnp.concatenate([xp_ref[0], x_ref[...], xn_ref[0]], axis=0)
    h_ext = _ada_pre(x_ext, vec_ref).astype(_BF)
    hb = h_ext[HALO:HALO + tm]

    pa = _dot(hb, watt_ref[...])
    reps = ATT_WIDTH // LANES
    cos = jnp.tile(cos_ref[...], (1, reps))
    sin = jnp.tile(sin_ref[...], (1, reps))
    q = pa[:, :ATT_WIDTH]
    k = pa[:, ATT_WIDTH:2 * ATT_WIDTH]
    _store_by_class((q * cos + _swap_halves(q) * sin) * (ATT_HEAD_DIM ** -0.5), (q0_ref, q1_ref, q2_ref),
                    stage_ref.at[0], tm)
    _store_by_class(k * cos + _swap_halves(k) * sin, (k0_ref, k1_ref, k2_ref), stage_ref.at[1], tm)
    _store_by_class(pa[:, 2 * ATT_WIDTH:], (v0_ref, v1_ref, v2_ref), stage_ref.at[2], tm)

    pdn = _dot(h_ext, wdn_ref[...])
    row = lax.broadcasted_iota(jnp.int32, (tm + 2 * HALO, 1), 0)
    outside = ((i == 0) & (row < HALO)) | ((i == last) & (row >= HALO + tm))
    pdn_ref[...] = jnp.where(outside, 0.0, pdn)
    pad = DN_CONV // 2
    acc = pdn_ref[HALO - pad:HALO - pad + tm, :] * conv_ref[0:1, :]
    for t in range(1, DN_CONV):
        acc = acc + pdn_ref[HALO - pad + t:HALO - pad + t + tm, :] * conv_ref[t:t + 1, :]
    dn = acc * jax.nn.sigmoid(acc)
    for hd in range(DN_HEADS):
        sl = slice(hd * DN_HEAD_DIM, (hd + 1) * DN_HEAD_DIM)
        qh = dn[:, sl]
        kh = dn[:, DN_WIDTH + hd * DN_HEAD_DIM:DN_WIDTH + (hd + 1) * DN_HEAD_DIM]
        qn = qh * lax.rsqrt(jnp.sum(qh * qh, axis=-1, keepdims=True) + EPS) * (DN_HEAD_DIM ** -0.5)
        kn = kh * lax.rsqrt(jnp.sum(kh * kh, axis=-1, keepdims=True) + EPS)
        qd_ref[:, sl] = qn.astype(_BF)
        kd_ref[:, sl] = kn.astype(_BF)
    vd_ref[...] = dn[:, 2 * DN_WIDTH:].astype(_BF)

    z = _dot(hb, wz_ref[...])
    z_ref[...] = (z * jax.nn.sigmoid(z)).astype(_BF)
    gate_ref[...] = jax.nn.sigmoid(_dot(hb, wg_ref[...])).astype(_BF)

    raw = _dot(hb, wba_ref[...])
    lane = lax.broadcasted_iota(jnp.int32, raw.shape, 1)
    beta = jax.nn.sigmoid(raw)
    log_decay = dec_ref[0:1, :] * jax.nn.softplus(raw + dec_ref[1:2, :])
    cum_f, cum_b = _chunk_cumsums(log_decay, tm)
    nh = DN_HEADS
    tab = jnp.where(lane < 2 * nh, beta, jnp.where(lane < 3 * nh, cum_f, jnp.where(lane < 4 * nh, cum_b, 0.0)))
    tab_ref[...] = tab
    tabt_ref[...] = tab.T[:4 * nh, :]


def _mixer_projection(x, vec, cos_t, sin_t, wts, conv_w, dec):
    s, d = x.shape
    tm = min(ROW_TILE, s)
    nt = s // tm
    hb = tm // HALO
    x3 = x.reshape(s // HALO, HALO, d)
    row = lambda i: (i, 0)
    w_att, w_dn, w_z, w_g, w_ba = wts
    bf = lambda n: jax.ShapeDtypeStruct((s, n), _BF)
    gw = ATT_GROUP_WIDTH
    att_specs = [pl.BlockSpec((tm // dil, dil * gw), row) for _, dil in ATT_GROUPS] * 3
    att_shapes = [jax.ShapeDtypeStruct((s // dil, dil * gw), _BF) for _, dil in ATT_GROUPS] * 3
    outs = pl.pallas_call(
        functools.partial(_proj_kernel, tm=tm),
        grid=(nt,),
        in_specs=[
            pl.BlockSpec((1, HALO, d), lambda i: (jnp.maximum(i * hb - 1, 0), 0, 0)),
            pl.BlockSpec((tm, d), row),
            pl.BlockSpec((1, HALO, d), lambda i: (jnp.minimum((i + 1) * hb, s // HALO - 1), 0, 0)),
            _resident(vec.shape),
            pl.BlockSpec((tm, LANES), row),
            pl.BlockSpec((tm, LANES), row),
            _resident(w_att.shape), _resident(w_dn.shape), _resident(w_z.shape), _resident(w_g.shape),
            _resident(w_ba.shape), _resident(conv_w.shape), _resident(dec.shape),
        ],
        out_specs=att_specs + [
            pl.BlockSpec((tm, DN_WIDTH), row), pl.BlockSpec((tm, DN_WIDTH), row), pl.BlockSpec((tm, DN_WIDTH), row),
            pl.BlockSpec((tm, DN_WIDTH), row), pl.BlockSpec((tm, 2 * d), row),
            pl.BlockSpec((tm, LANES), row), pl.BlockSpec((4 * DN_HEADS, tm), lambda i: (0, i)),
        ],
        out_shape=att_shapes + [bf(DN_WIDTH), bf(DN_WIDTH), bf(DN_WIDTH), bf(DN_WIDTH), bf(2 * d),
                                jax.ShapeDtypeStruct((s, LANES), _F32), jax.ShapeDtypeStruct((4 * DN_HEADS, s), _F32)],
        scratch_shapes=[pltpu.VMEM((tm + 2 * HALO, 3 * DN_WIDTH), _F32),
                        pltpu.VMEM((3, gw // LANES, tm, LANES), _F32)],
        compiler_params=_cparams("arbitrary"),
    )(x3, x, x3, vec, cos_t, sin_t, w_att, w_dn, w_z, w_g, w_ba, conv_w, dec)
    return outs[0:3], outs[3:6], outs[6:9], outs[9:]


def _att_kernel(q_ref, kp_ref, k_ref, kn_ref, vp_ref, v_ref, vn_ref, o_ref, lse_ref, kbuf, vbuf, *, nq, n_rows):
    i = pl.program_id(1)
    r = ATT_RADIUS
    kbuf[0:r, :] = kp_ref[...]
    kbuf[r:r + nq, :] = k_ref[...]
    kbuf[r + nq:, :] = kn_ref[...]
    vbuf[0:r, :] = vp_ref[...]
    vbuf[r:r + nq, :] = v_ref[...]
    vbuf[r + nq:, :] = vn_ref[...]

    qs = ATT_Q_SUB
    kw = qs + 2 * r
    qi = lax.broadcasted_iota(jnp.int32, (qs, kw), 0)
    kj = lax.broadcasted_iota(jnp.int32, (qs, kw), 1) - r
    band = jnp.where(jnp.abs(kj - qi) <= r, 0.0, NEG_INF)
    lane = lax.broadcasted_iota(jnp.int32, (1, LANES), 1)
    low = lane < ATT_HEAD_DIM
    kcol = lax.broadcasted_iota(jnp.int32, (1, kw), 1)
    for j in range(nq // qs):
        r0 = j * qs
        kidx = i * nq + r0 - r + kcol
        edge = jnp.where((kidx < 0) | (kidx >= n_rows), NEG_INF, 0.0)
        bias = band + edge
        bias2 = jnp.concatenate([bias, bias], axis=0)
        for hp in range(ATT_GROUP_WIDTH // LANES):
            cs = slice(hp * LANES, (hp + 1) * LANES)
            qp = q_ref[r0:r0 + qs, cs]
            kp = kbuf[r0:r0 + kw, cs]
            vp = vbuf[r0:r0 + kw, cs]
            zero = jnp.zeros_like(qp)
            qq = jnp.concatenate([jnp.where(low, qp, zero), jnp.where(low, zero, qp)], axis=0)
            sc = _dot_nt(qq, kp) + bias2
            m = jnp.max(sc, axis=-1, keepdims=True)
            p = jnp.exp(sc - m)
            den = jnp.sum(p, axis=-1, keepdims=True)
            pv = _dot(p.astype(_BF), vp) / den
            lse = m + jnp.log(den)
            o_ref[r0:r0 + qs, cs] = jnp.where(low, pv[:qs], pv[qs:]).astype(_BF)
            lse_ref[r0:r0 + qs, cs] = jnp.where(low, lse[:qs], lse[qs:])


def _dilated_attention(qv, kv, vv, dilation):
    n_rows = qv.shape[0]
    nq = min(ATT_Q_TILE, n_rows)
    nt = n_rows // nq
    r = ATT_RADIUS
    gw = ATT_GROUP_WIDTH
    main = pl.BlockSpec((nq, gw), lambda c, i: (i, c))
    prev = pl.BlockSpec((r, gw), lambda c, i: (jnp.maximum(i * (nq // r) - 1, 0), c))
    nxt = pl.BlockSpec((r, gw), lambda c, i: (jnp.minimum((i + 1) * (nq // r), n_rows // r - 1), c))
    out = main
    return pl.pallas_call(
        functools.partial(_att_kernel, nq=nq, n_rows=n_rows),
        grid=(dilation, nt),
        in_specs=[main, prev, main, nxt, prev, main, nxt],
        out_specs=[out, out],
        out_shape=[jax.ShapeDtypeStruct((n_rows, dilation * gw), _BF),
                   jax.ShapeDtypeStruct((n_rows, dilation * gw), _F32)],
        scratch_shapes=[pltpu.VMEM((nq + 2 * r, gw), _BF), pltpu.VMEM((nq + 2 * r, gw), _BF)],
        compiler_params=_cparams("arbitrary", "arbitrary"),
    )(qv, kv, kv, kv, vv, vv, vv)


def _block_diag_inverses(ms):
    n = len(ms)
    c = ms[0].shape[0]
    b = DN_BASE
    nb = c // b
    lane = lax.broadcasted_iota(jnp.int32, (b, c), 1)
    sub = lax.broadcasted_iota(jnp.int32, (b, c), 0)
    blk = lane // b
    dgs = []
    for m in ms:
        dg = jnp.zeros((b, c), _F32)
        for bi in range(nb):
            dg = jnp.where(blk == bi, m[bi * b:(bi + 1) * b, :], dg)
        dgs.append(dg)
    dg_all = jnp.concatenate(dgs, axis=0)
    blk_all = lax.broadcasted_iota(jnp.int32, (n * b, c), 1) // b
    cols = [jnp.take_along_axis(dg_all, blk_all * b + j, axis=1).reshape(n, b, c) for j in range(b)]
    eye = jnp.broadcast_to(jnp.where(sub == lane % b, 1.0, 0.0), (n // 2, b, c))
    lo, up = eye, eye
    for t in range(b - 1):
        j = t
        lo = lo - cols[j][:n // 2] * lo[:, j:j + 1, :]
        j = b - 1 - t
        up = up - cols[j][n // 2:] * up[:, j:j + 1, :]
    out = []
    for i in range(n):
        acc = lo[i] if i < n // 2 else up[i - n // 2]
        out.append(jnp.concatenate([jnp.where(blk == bi, acc, 0.0) for bi in range(nb)], axis=0))
    return out


def _dn_kernel(qf_ref, kf_ref, vf_ref, tf_ref, ttf_ref, qb_ref, kb_ref, vb_ref, tb_ref, ttb_ref,
               of_ref, ob_ref, state_ref):
    @pl.when(pl.program_id(0) == 0)
    def _():
        state_ref[...] = jnp.zeros_like(state_ref)

    c = DN_CHUNK
    nh = DN_HEADS
    ri = lax.broadcasted_iota(jnp.int32, (c, c), 0)
    ci = lax.broadcasted_iota(jnp.int32, (c, c), 1)
    dirs = ((qf_ref, kf_ref, vf_ref, tf_ref, ttf_ref, of_ref), (qb_ref, kb_ref, vb_ref, tb_ref, ttb_ref, ob_ref))

    inst = []
    for direction, (q_ref, k_ref, v_ref, t_ref, tt_ref, o_ref) in enumerate(dirs):
        upper = direction == 1
        incl = (ri <= ci) if upper else (ri >= ci)
        strict = (ri < ci) if upper else (ri > ci)
        tab = t_ref[...]
        tabt = tt_ref[...]
        for hd in range(nh):
            sl = slice(hd * DN_HEAD_DIM, (hd + 1) * DN_HEAD_DIM)
            kb = k_ref[:, sl]
            lb = direction * nh + hd
            lg = 2 * nh + lb
            beta = tab[:, lb:lb + 1]
            gcol = tab[:, lg:lg + 1]
            grow = tabt[lg:lg + 1, :]
            decay = jnp.exp(jnp.where(incl, gcol - grow, NEG_INF))
            m = jnp.where(strict, _dot_nt(kb, kb) * beta * decay, 0.0)
            attn = (_dot_nt(q_ref[:, sl], kb) * decay).astype(_BF)
            inst.append((direction, hd, m, attn, beta, gcol))

    xs = _block_diag_inverses([it[2] for it in inst])

    n = len(inst)
    ms = [it[2] for it in inst]
    size = DN_BASE
    while size < c:
        off_diag = (ri // (2 * size) == ci // (2 * size)) & (ri // size != ci // size)
        xbs = [x.astype(_BF) for x in xs]
        ts = [_dot(xbs[i], jnp.where(off_diag, ms[i], 0.0).astype(_BF)).astype(_BF) for i in range(n)]
        xs = [xs[i] - _dot(ts[i], xbs[i]) for i in range(n)]
        size *= 2

    def cols(i):
        return slice(inst[i][1] * DN_HEAD_DIM, (inst[i][1] + 1) * DN_HEAD_DIM)

    sols, qgs, kdecs, gls = [], [], [], []
    for i, (direction, hd, m, attn, beta, gcol) in enumerate(inst):
        q_ref, k_ref, v_ref = dirs[direction][:3]
        q, k, v = (r[:, cols(i)].astype(_F32) for r in (q_ref, k_ref, v_ref))
        glast = gcol[0:1, :] if direction == 1 else gcol[c - 1:c, :]
        eg = jnp.exp(gcol)
        rhs = jnp.concatenate([v * beta, k * (beta * eg)], axis=1).astype(_BF)
        sols.append(_dot(xs[i].astype(_BF), rhs))
        qgs.append(q * eg)
        kdecs.append((k * jnp.exp(glast - gcol)).astype(_BF))
        gls.append(jnp.exp(glast))
    states = [state_ref[i] for i in range(n)]
    wss = [_dot(jnp.concatenate([sols[i][:, DN_HEAD_DIM:], qgs[i]], axis=0).astype(_BF), states[i].astype(_BF))
           for i in range(n)]
    vbs = [(sols[i][:, :DN_HEAD_DIM] - wss[i][:c]).astype(_BF) for i in range(n)]
    for i in range(n):
        dirs[inst[i][0]][5][:, cols(i)] = wss[i][c:] + _dot(inst[i][3], vbs[i])
    for i in range(n):
        state_ref[i] = states[i] * gls[i] + _dot_tn(kdecs[i], vbs[i])


def _gated_delta(qd, kd, vd, tab, tabt):
    s = qd.shape[0]
    c = DN_CHUNK
    n = s // c
    fwd = lambda i: (i, 0)
    bwd = lambda i: (n - 1 - i, 0)
    wide = lambda im: pl.BlockSpec((c, DN_WIDTH), im)
    tspec = lambda im: pl.BlockSpec((c, LANES), im)
    ttspec = lambda flip: pl.BlockSpec((4 * DN_HEADS, c), (lambda i: (0, n - 1 - i)) if flip else (lambda i: (0, i)))
    return pl.pallas_call(
        _dn_kernel,
        grid=(n,),
        in_specs=[wide(fwd), wide(fwd), wide(fwd), tspec(fwd), ttspec(False),
                  wide(bwd), wide(bwd), wide(bwd), tspec(bwd), ttspec(True)],
        out_specs=[wide(fwd), wide(bwd)],
        out_shape=[jax.ShapeDtypeStruct((s, DN_WIDTH), _F32)] * 2,
        scratch_shapes=[pltpu.VMEM((2 * DN_HEADS, DN_HEAD_DIM, DN_HEAD_DIM), _F32)],
        compiler_params=_cparams("arbitrary"),
    )(qd, kd, vd, tab, tabt, qd, kd, vd, tab, tabt)


def _token_order(blk_ref, stage_ref, dil, tm):
    gw = ATT_GROUP_WIDTH
    if dil == 1:
        return blk_ref[...].astype(_F32)
    for r in range(dil):
        for c in range(gw // LANES):
            stage_ref[c, pl.ds(r, tm // dil, stride=dil), :] = (
                blk_ref[:, r * gw + c * LANES:r * gw + (c + 1) * LANES].astype(_F32))
    return jnp.concatenate([stage_ref[c] for c in range(gw // LANES)], axis=1)


def _out_kernel(x_ref, o0_ref, o1_ref, o2_ref, l0_ref, l1_ref, l2_ref, of_ref, ob_ref, z_ref, gate_ref, vec_ref,
                wa_ref, wd_ref, wo_ref, out_ref, odn_ref, stage_ref, *, tm):
    d = D_MODEL
    dils = [dil for _, dil in ATT_GROUPS]
    o0, o1, o2 = (_token_order(r, stage_ref.at[n], dils[n], tm) for n, r in enumerate((o0_ref, o1_ref, o2_ref)))
    l0, l1, l2 = (_token_order(r, stage_ref.at[3 + n], dils[n], tm) for n, r in enumerate((l0_ref, l1_ref, l2_ref)))
    mx = jnp.maximum(jnp.maximum(l0, l1), l2)
    e0, e1, e2 = jnp.exp(l0 - mx), jnp.exp(l1 - mx), jnp.exp(l2 - mx)
    o_att = (e0 * o0 + e1 * o1 + e2 * o2) / (e0 + e1 + e2)
    for hd in range(DN_HEADS):
        sl = slice(hd * DN_HEAD_DIM, (hd + 1) * DN_HEAD_DIM)
        o = of_ref[:, sl] + ob_ref[:, sl]
        o = _rms(o) * vec_ref[5:6, 0:DN_HEAD_DIM]
        odn_ref[:, sl] = (o * z_ref[:, sl].astype(_F32)).astype(_BF)
    ya = _dot(o_att.astype(_BF), wa_ref[...])
    yd = _dot(odn_ref[...], wd_ref[...])
    y = gate_ref[:, :d].astype(_F32) * ya + gate_ref[:, d:].astype(_F32) * yd
    y = _dot(y.astype(_BF), wo_ref[...])
    out_ref[...] = x_ref[...] + vec_ref[3:4, :] * (_rms(y) * vec_ref[4:5, :])


def _mixer_output(x, att, o_f, o_b, z, gates, vec, w_a, w_d, w_o):
    s, d = x.shape
    tm = min(ROW_TILE, s)
    row = lambda i: (i, 0)
    (o0, l0), (o1, l1), (o2, l2) = att
    gw = ATT_GROUP_WIDTH
    by_class = [pl.BlockSpec((tm // dil, dil * gw), row) for _, dil in ATT_GROUPS]
    return pl.pallas_call(
        functools.partial(_out_kernel, tm=tm),
        grid=(s // tm,),
        in_specs=[pl.BlockSpec((tm, d), row)] + by_class * 2
        + [pl.BlockSpec((tm, DN_WIDTH), row)] * 3 + [pl.BlockSpec((tm, 2 * d), row)]
        + [_resident(vec.shape), _resident(w_a.shape), _resident(w_d.shape), _resident(w_o.shape)],
        out_specs=pl.BlockSpec((tm, d), row),
        out_shape=jax.ShapeDtypeStruct((s, d), _F32),
        scratch_shapes=[pltpu.VMEM((tm, DN_WIDTH), _BF), pltpu.VMEM((6, gw // LANES, tm, LANES), _F32)],
        compiler_params=_cparams("arbitrary"),
    )(x, o0, o1, o2, l0, l1, l2, o_f, o_b, z, gates, vec, w_a, w_d, w_o)


def _vec(rows, d):
    pad = [jnp.zeros((d,), _F32)] * (SUBLANES - len(rows))
    return jnp.stack(list(rows) + pad, axis=0)


def kernel(x, c, positions, w_ada, b_ada, norm_pre, norm_post, ffn_w_in, ffn_w_out, w_in, conv_w, a_log, dt_bias,
           dn_norm_w, w_branch_att, w_branch_dn, w_out):
    b, s, d = x.shape
    assert b == 1 and d == D_MODEL and s % (DN_CHUNK * 16) == 0
    depth = w_ada.shape[0]
    x2 = x.reshape(s, d)
    mod = _ada_modulation(c, w_ada, b_ada)
    cos_t, sin_t = _rope_tables(positions)

    o_dn = 3 * ATT_WIDTH
    o_z = o_dn + 3 * DN_WIDTH
    o_ba = o_z + DN_WIDTH
    o_g = o_ba + 4 * DN_HEADS
    nh = DN_HEADS
    zeros_row = jnp.zeros((LANES,), _F32)

    for l in range(depth):
        def vec_for(sub, extra=None):
            rows = [norm_pre[l, sub], mod[l, sub, 0], mod[l, sub, 1], mod[l, sub, 2], norm_post[l, sub]]
            if extra is not None:
                rows.append(extra)
            return _vec(rows, d)

        x2 = _ffn_sublayer(x2, vec_for(0), ffn_w_in[l, 0].astype(_BF), ffn_w_out[l, 0].astype(_BF))

        wl = w_in[l]
        w_ba = jnp.pad(wl[:, o_ba:o_g], ((0, 0), (0, LANES - 4 * nh))).astype(_BF)
        wts = (wl[:, :o_dn].astype(_BF), wl[:, o_dn:o_z].astype(_BF), wl[:, o_z:o_ba].astype(_BF),
               wl[:, o_g:].astype(_BF), w_ba)
        neg_a = zeros_row.at[2 * nh:4 * nh].set(-jnp.exp(a_log[l].astype(_F32)).reshape(-1))
        dtb = zeros_row.at[2 * nh:4 * nh].set(dt_bias[l].astype(_F32).reshape(-1))
        dec = _vec([neg_a, dtb], LANES)
        qa, ka, va, (qd, kd, vd, z, gates, tab, tabt) = _mixer_projection(
            x2, vec_for(1), cos_t, sin_t, wts, conv_w[l], dec)
        att = [_dilated_attention(qa[g], ka[g], va[g], dil) for g, (_, dil) in enumerate(ATT_GROUPS)]
        o_f, o_b = _gated_delta(qd, kd, vd, tab, tabt)
        x2 = _mixer_output(x2, att, o_f, o_b, z, gates, vec_for(1, jnp.tile(dn_norm_w[l], d // DN_HEAD_DIM)),
                           w_branch_att[l].astype(_BF), w_branch_dn[l].astype(_BF), w_out[l].astype(_BF))

        x2 = _ffn_sublayer(x2, vec_for(2), ffn_w_in[l, 1].astype(_BF), ffn_w_out[l, 1].astype(_BF))
    return x2.reshape(b, s, d)
```

```python
import functools
import itertools

import jax
import jax.numpy as jnp
from jax import lax
from jax.experimental import pallas as pl
from jax.experimental.pallas import tpu as pltpu

D_MODEL = 1024
DEPTH = 2
N_SUBLAYERS = 3
ATT_GROUPS = ((128, 1), (512, 4), (2048, 16))
ATT_HEADS_PER_GROUP = 4
ATT_HEAD_DIM = 64
ATT_WIDTH = len(ATT_GROUPS) * ATT_HEADS_PER_GROUP * ATT_HEAD_DIM
ATT_GROUP_WIDTH = ATT_HEADS_PER_GROUP * ATT_HEAD_DIM
ATT_RADIUS = 64
ROPE_THETA = 10000.0
DN_HEADS = 6
DN_HEAD_DIM = 128
DN_WIDTH = DN_HEADS * DN_HEAD_DIM
DN_CONV = 5
FFN_DIM = 2816
EPS = 1e-6
NEG_INF = -1e30

LANES = 128
SUBLANES = 8
BF16_ROWS = 16
MXU_DIM = 256
VMEM_LIMIT_BYTES = 56 * 1024 * 1024

ROW_TILE = 512
FFN_CHUNKS = 2
FFN_ROW_TILE = 1024
FFN_ROW_SPLIT = 2
OUT_ROW_SPLIT = 2
PROJ_CHUNK = 256
ATT_Q_TILE = 1024
ATT_Q_SUB = 128
DN_CHUNK = 128
DN_BASE = 16
DN_GROUP = 6
HALO = BF16_ROWS

_BF = jnp.bfloat16
_F32 = jnp.float32


def _cparams(*sem):
    return pltpu.CompilerParams(dimension_semantics=sem, vmem_limit_bytes=VMEM_LIMIT_BYTES)


def _resident(shape):
    nd = len(shape)
    return pl.BlockSpec(shape, lambda *_: (0,) * nd, pipeline_mode=pl.Buffered(1))


def _rms(t):
    return t * lax.rsqrt(jnp.mean(t * t, axis=-1, keepdims=True) + EPS)


def _dot(a, b):
    return jnp.dot(a, b, preferred_element_type=_F32)


def _dot_nt(a, b):
    return lax.dot_general(a, b, (((1,), (1,)), ((), ())), preferred_element_type=_F32)


def _dot_tn(a, b):
    return lax.dot_general(a, b, (((0,), (0,)), ((), ())), preferred_element_type=_F32)


def _ada_kernel(c_ref, w_ref, b_ref, o_ref):
    c = c_ref[...]
    s = c * jax.nn.sigmoid(c)
    o_ref[0] = jnp.sum(w_ref[0] * s, axis=0, keepdims=True) + b_ref[0]


def _ada_modulation(c, w_ada, b_ada):
    depth, d, n = w_ada.shape
    tn = 1024
    out = pl.pallas_call(
        _ada_kernel,
        grid=(depth, n // tn),
        in_specs=[
            pl.BlockSpec((d, 1), lambda l, j: (0, 0)),
            pl.BlockSpec((1, d, tn), lambda l, j: (l, 0, j)),
            pl.BlockSpec((1, 1, tn), lambda l, j: (l, 0, j)),
        ],
        out_specs=pl.BlockSpec((1, 1, tn), lambda l, j: (l, 0, j)),
        out_shape=jax.ShapeDtypeStruct((depth, 1, n), _F32),
        compiler_params=_cparams("arbitrary", "arbitrary"),
    )(c.reshape(d, 1), w_ada, b_ada.reshape(depth, 1, n))
    return out.reshape(depth, N_SUBLAYERS, 3, d)


def _rope_kernel(pos_ref, inv_ref, cos_ref, sin_ref):
    ang = pos_ref[...].astype(_F32) * inv_ref[...]
    lane = lax.broadcasted_iota(jnp.int32, ang.shape, 1)
    first_half = (lane % ATT_HEAD_DIM) < (ATT_HEAD_DIM // 2)
    cos_ref[...] = jnp.cos(ang)
    sin_ref[...] = jnp.where(first_half, -jnp.sin(ang), jnp.sin(ang))


def _rope_tables(positions):
    s = positions.shape[1]
    ts = min(2048, s)
    half = ATT_HEAD_DIM // 2
    inv = ROPE_THETA ** (-jnp.arange(half, dtype=_F32) * 2.0 / ATT_HEAD_DIM)
    inv_row = jnp.tile(inv, LANES // half).reshape(1, LANES)
    return pl.pallas_call(
        _rope_kernel,
        grid=(s // ts,),
        in_specs=[pl.BlockSpec((ts, 1), lambda i: (i, 0)), pl.BlockSpec((1, LANES), lambda i: (0, 0))],
        out_specs=[pl.BlockSpec((ts, LANES), lambda i: (i, 0))] * 2,
        out_shape=[jax.ShapeDtypeStruct((s, LANES), _F32)] * 2,
        compiler_params=_cparams("arbitrary"),
    )(positions.reshape(s, 1), inv_row)


def _ada_pre(x, vec_ref):
    a = vec_ref[0:1, :] * (1.0 + vec_ref[2:3, :])
    return _rms(x) * a + vec_ref[1:2, :]


def _ffn_kernel(x_ref, vec_ref, win_ref, wout_ref, o_ref, h_ref, act_ref, *, tm):
    rows = [pl.ds(r * (tm // FFN_ROW_SPLIT), tm // FFN_ROW_SPLIT) for r in range(FFN_ROW_SPLIT)]
    tiles = FFN_DIM // MXU_DIM
    bounds = [MXU_DIM * (j * tiles // FFN_CHUNKS) for j in range(FFN_CHUNKS)] + [FFN_DIM]
    for r in rows:
        h_ref[r, :] = _ada_pre(x_ref[r, :], vec_ref).astype(_BF)
    for lo, hi in zip(bounds[:-1], bounds[1:]):
        for r in rows:
            g = _dot(h_ref[r, :], win_ref[:, lo:hi])
            u = _dot(h_ref[r, :], win_ref[:, FFN_DIM + lo:FFN_DIM + hi])
            act_ref[r, lo:hi] = (g * jax.nn.sigmoid(g) * u).astype(_BF)
    for r in rows:
        y = _dot(act_ref[r, :], wout_ref[...])
        o_ref[r, :] = x_ref[r, :] + (0.5 * vec_ref[3:4, :]) * (_rms(y) * vec_ref[4:5, :])


def _ffn_sublayer(x, vec, w_in, w_out):
    s, d = x.shape
    tm = min(FFN_ROW_TILE, s)
    return pl.pallas_call(
        functools.partial(_ffn_kernel, tm=tm),
        grid=(s // tm,),
        in_specs=[
            pl.BlockSpec((tm, d), lambda i: (i, 0)),
            _resident(vec.shape),
            _resident(w_in.shape),
            _resident(w_out.shape),
        ],
        out_specs=pl.BlockSpec((tm, d), lambda i: (i, 0)),
        out_shape=jax.ShapeDtypeStruct((s, d), _F32),
        scratch_shapes=[pltpu.VMEM((tm, d), _BF), pltpu.VMEM((tm, FFN_DIM), _BF)],
        compiler_params=_cparams("arbitrary"),
    )(x, vec, w_in, w_out)


def _swap_halves(t):
    n = t.shape[1]
    half = ATT_HEAD_DIM // 2
    from_right = pltpu.roll(t, n - half, 1)
    from_left = pltpu.roll(t, half, 1)
    lane = lax.broadcasted_iota(jnp.int32, t.shape, 1)
    return jnp.where((lane % ATT_HEAD_DIM) < half, from_right, from_left)


def _chunk_cumsums(g, tm):
    row = lax.broadcasted_iota(jnp.int32, g.shape, 0) % DN_CHUNK
    fwd, bwd = g, g
    sh = 1
    while sh < DN_CHUNK:
        fwd = fwd + jnp.where(row >= sh, pltpu.roll(fwd, sh, 0), 0.0)
        bwd = bwd + jnp.where(row < DN_CHUNK - sh, pltpu.roll(bwd, tm - sh, 0), 0.0)
        sh *= 2
    return fwd, bwd


def _store_by_class(val, out_ref, stage_ref, dil, tm):
    gw = ATT_GROUP_WIDTH
    if dil == 1:
        out_ref[...] = val.astype(_BF)
        return
    for c in range(gw // LANES):
        stage_ref[c] = val[:, c * LANES:(c + 1) * LANES]
    for r in range(dil):
        for c in range(gw // LANES):
            out_ref[:, r * gw + c * LANES:r * gw + (c + 1) * LANES] = (
                stage_ref[c, pl.ds(r, tm // dil, stride=dil), :].astype(_BF))


def _proj_kernel(xp_ref, x_ref, xn_ref, vec_ref, cos_ref, sin_ref, watt_ref, wdn_ref, wz_ref, wg_ref, wba_ref,
                 conv_ref, dec_ref,
                 q0_ref, q1_ref, q2_ref, k0_ref, k1_ref, k2_ref, v0_ref, v1_ref, v2_ref,
                 qd_ref, kd_ref, vd_ref, z_ref, gate_ref, tab_ref, tabt_ref,
                 h_ref, pdn_ref, stage_ref, *, tm):
    i = pl.program_id(0)
    last = pl.num_programs(0) - 1
    cw = PROJ_CHUNK
    gw = ATT_GROUP_WIDTH
    x_ext = jnp.concatenate([xp_ref[0], x_ref[...], xn_ref[0]], axis=0)
    h_ref[...] = _ada_pre(x_ext, vec_ref).astype(_BF)
    inner = pl.ds(HALO, tm)
    cos = jnp.tile(cos_ref[...], (1, cw // LANES))
    sin = jnp.tile(sin_ref[...], (1, cw // LANES))
    row = lax.broadcasted_iota(jnp.int32, (tm + 2 * HALO, 1), 0)
    outside = ((i == 0) & (row < HALO)) | ((i == last) & (row >= HALO + tm))
    pad = DN_CONV // 2
    att_out = ((q0_ref, q1_ref, q2_ref), (k0_ref, k1_ref, k2_ref), (v0_ref, v1_ref, v2_ref))
    dn_out = (qd_ref, kd_ref, vd_ref)

    def att_task(kind, g):
        col = kind * ATT_WIDTH + g * gw

        def epilogue(t):
            if kind < 2:
                t = t * cos + _swap_halves(t) * sin
            if kind == 0:
                t = t * (ATT_HEAD_DIM ** -0.5)
            _store_by_class(t, att_out[kind][g], stage_ref.at[kind], ATT_GROUPS[g][1], tm)
        return (lambda: _dot(h_ref[inner, :], watt_ref[:, col:col + cw])), epilogue

    def dn_task(c):
        kind, col = divmod(c * cw, DN_WIDTH)
        slot = c % 2

        def epilogue(t):
            p = jnp.where(outside, 0.0, t)
            wc = conv_ref[:, c * cw:(c + 1) * cw]
            acc = p[HALO:HALO + tm] * wc[pad:pad + 1, :]
            for tap in range(DN_CONV):
                if tap != pad:
                    shifted = pltpu.roll(p, (pad - tap) % (tm + 2 * HALO), 0)
                    acc = acc + shifted[HALO:HALO + tm] * wc[tap:tap + 1, :]
            dn = acc * jax.nn.sigmoid(acc)
            for hd in range(cw // DN_HEAD_DIM):
                t_h = dn[:, hd * DN_HEAD_DIM:(hd + 1) * DN_HEAD_DIM]
                if kind < 2:
                    t_h = t_h * lax.rsqrt(jnp.sum(t_h * t_h, axis=-1, keepdims=True) + EPS)
                if kind == 0:
                    t_h = t_h * (DN_HEAD_DIM ** -0.5)
                dn_out[kind][:, col + hd * DN_HEAD_DIM:col + (hd + 1) * DN_HEAD_DIM] = t_h.astype(_BF)
        return (lambda: _dot(h_ref[...], wdn_ref[:, c * cw:(c + 1) * cw])), epilogue

    def z_task(c):
        def epilogue(t):
            z_ref[:, c * cw:(c + 1) * cw] = (t * jax.nn.sigmoid(t)).astype(_BF)
        return (lambda: _dot(h_ref[inner, :], wz_ref[:, c * cw:(c + 1) * cw])), epilogue

    def gate_task(c):
        def epilogue(t):
            gate_ref[:, c * cw:(c + 1) * cw] = jax.nn.sigmoid(t).astype(_BF)
        return (lambda: _dot(h_ref[inner, :], wg_ref[:, c * cw:(c + 1) * cw])), epilogue

    def table_task():
        def epilogue(raw):
            lane = lax.broadcasted_iota(jnp.int32, raw.shape, 1)
            beta = jax.nn.sigmoid(raw)
            log_decay = dec_ref[0:1, :] * jax.nn.softplus(raw + dec_ref[1:2, :])
            cum_f, cum_b = _chunk_cumsums(log_decay, tm)
            nh = DN_HEADS
            tab = jnp.where(lane < 2 * nh, beta, jnp.where(lane < 3 * nh, cum_f, jnp.where(lane < 4 * nh, cum_b, 0.0)))
            tab_ref[...] = tab
            tabt_ref[...] = tab.T[:4 * nh, :]
        return (lambda: _dot(h_ref[inner, :], wba_ref[...])), epilogue

    light = ([att_task(kind, g) for kind in range(3) for g in range(len(ATT_GROUPS))]
             + [z_task(c) for c in range(DN_WIDTH // cw)] + [gate_task(c) for c in range(2 * D_MODEL // cw)]
             + [table_task()])
    heavy = [dn_task(c) for c in range(3 * DN_WIDTH // cw)]
    tasks = []
    for n, task in enumerate(heavy):
        tasks += [task] + light[n * len(light) // len(heavy):(n + 1) * len(light) // len(heavy)]
    pending = None
    for matmul, epilogue in tasks:
        val = matmul()
        if pending is not None:
            pending[1](pending[0])
        pending = (val, epilogue)
    pending[1](pending[0])


def _mixer_projection(x, vec, cos_t, sin_t, wts, conv_w, dec):
    s, d = x.shape
    tm = min(ROW_TILE, s)
    nt = s // tm
    hb = tm // HALO
    x3 = x.reshape(s // HALO, HALO, d)
    row = lambda i: (i, 0)
    w_att, w_dn, w_z, w_g, w_ba = wts
    bf = lambda n: jax.ShapeDtypeStruct((s, n), _BF)
    gw = ATT_GROUP_WIDTH
    att_specs = [pl.BlockSpec((tm // dil, dil * gw), row) for _, dil in ATT_GROUPS] * 3
    att_shapes = [jax.ShapeDtypeStruct((s // dil, dil * gw), _BF) for _, dil in ATT_GROUPS] * 3
    outs = pl.pallas_call(
        functools.partial(_proj_kernel, tm=tm),
        grid=(nt,),
        in_specs=[
            pl.BlockSpec((1, HALO, d), lambda i: (jnp.maximum(i * hb - 1, 0), 0, 0)),
            pl.BlockSpec((tm, d), row),
            pl.BlockSpec((1, HALO, d), lambda i: (jnp.minimum((i + 1) * hb, s // HALO - 1), 0, 0)),
            _resident(vec.shape),
            pl.BlockSpec((tm, LANES), row),
            pl.BlockSpec((tm, LANES), row),
            _resident(w_att.shape), _resident(w_dn.shape), _resident(w_z.shape), _resident(w_g.shape),
            _resident(w_ba.shape), _resident(conv_w.shape), _resident(dec.shape),
        ],
        out_specs=att_specs + [
            pl.BlockSpec((tm, DN_WIDTH), row), pl.BlockSpec((tm, DN_WIDTH), row), pl.BlockSpec((tm, DN_WIDTH), row),
            pl.BlockSpec((tm, DN_WIDTH), row), pl.BlockSpec((tm, 2 * d), row),
            pl.BlockSpec((tm, LANES), row), pl.BlockSpec((4 * DN_HEADS, tm), lambda i: (0, i)),
        ],
        out_shape=att_shapes + [bf(DN_WIDTH), bf(DN_WIDTH), bf(DN_WIDTH), bf(DN_WIDTH), bf(2 * d),
                                jax.ShapeDtypeStruct((s, LANES), _F32), jax.ShapeDtypeStruct((4 * DN_HEADS, s), _F32)],
        scratch_shapes=[pltpu.VMEM((tm + 2 * HALO, d), _BF),
                        pltpu.VMEM((2, tm + 2 * HALO, PROJ_CHUNK), _F32),
                        pltpu.VMEM((3, gw // LANES, tm, LANES), _F32)],
        compiler_params=_cparams("arbitrary"),
    )(x3, x, x3, vec, cos_t, sin_t, w_att, w_dn, w_z, w_g, w_ba, conv_w, dec)
    return outs[0:3], outs[3:6], outs[6:9], outs[9:]


def _att_kernel(q_ref, kp_ref, k_ref, kn_ref, vp_ref, v_ref, vn_ref, o_ref, lse_ref, kbuf, vbuf, *, nq, n_rows):
    i = pl.program_id(1)
    r = ATT_RADIUS
    kbuf[0:r, :] = kp_ref[...]
    kbuf[r:r + nq, :] = k_ref[...]
    kbuf[r + nq:, :] = kn_ref[...]
    vbuf[0:r, :] = vp_ref[...]
    vbuf[r:r + nq, :] = v_ref[...]
    vbuf[r + nq:, :] = vn_ref[...]

    qs = ATT_Q_SUB
    kw = qs + 2 * r
    qi = lax.broadcasted_iota(jnp.int32, (qs, kw), 0)
    kj = lax.broadcasted_iota(jnp.int32, (qs, kw), 1) - r
    band = jnp.where(jnp.abs(kj - qi) <= r, 0.0, NEG_INF)
    lane = lax.broadcasted_iota(jnp.int32, (1, LANES), 1)
    low = lane < ATT_HEAD_DIM
    kcol = lax.broadcasted_iota(jnp.int32, (1, kw), 1)
    for j in range(nq // qs):
        r0 = j * qs
        kidx = i * nq + r0 - r + kcol
        edge = jnp.where((kidx < 0) | (kidx >= n_rows), NEG_INF, 0.0)
        bias = band + edge
        bias2 = jnp.concatenate([bias, bias], axis=0)
        for hp in range(ATT_GROUP_WIDTH // LANES):
            cs = slice(hp * LANES, (hp + 1) * LANES)
            qp = q_ref[r0:r0 + qs, cs]
            kp = kbuf[r0:r0 + kw, cs]
            vp = vbuf[r0:r0 + kw, cs]
            zero = jnp.zeros_like(qp)
            qq = jnp.concatenate([jnp.where(low, qp, zero), jnp.where(low, zero, qp)], axis=0)
            sc = _dot_nt(qq, kp) + bias2
            m = jnp.max(sc, axis=-1, keepdims=True)
            p = jnp.exp(sc - m)
            den = jnp.sum(p, axis=-1, keepdims=True)
            pv = _dot(p.astype(_BF), vp) / den
            lse = m + jnp.log(den)
            o_ref[r0:r0 + qs, cs] = jnp.where(low, pv[:qs], pv[qs:]).astype(_BF)
            lse_ref[r0:r0 + qs, cs] = jnp.where(low, lse[:qs], lse[qs:])


def _dilated_attention(qv, kv, vv, dilation):
    n_rows = qv.shape[0]
    nq = min(ATT_Q_TILE, n_rows)
    nt = n_rows // nq
    r = ATT_RADIUS
    gw = ATT_GROUP_WIDTH
    main = pl.BlockSpec((nq, gw), lambda c, i: (i, c))
    prev = pl.BlockSpec((r, gw), lambda c, i: (jnp.maximum(i * (nq // r) - 1, 0), c))
    nxt = pl.BlockSpec((r, gw), lambda c, i: (jnp.minimum((i + 1) * (nq // r), n_rows // r - 1), c))
    out = main
    return pl.pallas_call(
        functools.partial(_att_kernel, nq=nq, n_rows=n_rows),
        grid=(dilation, nt),
        in_specs=[main, prev, main, nxt, prev, main, nxt],
        out_specs=[out, out],
        out_shape=[jax.ShapeDtypeStruct((n_rows, dilation * gw), _BF),
                   jax.ShapeDtypeStruct((n_rows, dilation * gw), _F32)],
        scratch_shapes=[pltpu.VMEM((nq + 2 * r, gw), _BF), pltpu.VMEM((nq + 2 * r, gw), _BF)],
        compiler_params=_cparams("arbitrary", "arbitrary"),
    )(qv, kv, kv, kv, vv, vv, vv)


def _diag_blocks_by_lane(m):
    c = m.shape[0]
    b = DN_BASE
    blk = lax.broadcasted_iota(jnp.int32, (b, c), 1) // b
    dg = jnp.zeros((b, c), _F32)
    for bi in range(c // b):
        dg = jnp.where(blk == bi, m[bi * b:(bi + 1) * b, :], dg)
    return dg


def _blocks_to_diagonal(acc):
    b, c = acc.shape
    blk = lax.broadcasted_iota(jnp.int32, (b, c), 1) // b
    return jnp.concatenate([jnp.where(blk == bi, acc, 0.0) for bi in range(c // b)], axis=0)


def _pair_block_diag(t):
    c = t.shape[0]
    left = lax.broadcasted_iota(jnp.int32, (1, 2 * c), 1) < c
    zero = jnp.zeros_like(t)
    return jnp.concatenate([jnp.where(left, t, zero), jnp.where(left, zero, t)], axis=0)


def _dn_kernel(q3f_ref, k3f_ref, v3f_ref, t3f_ref, q1f_ref, k1f_ref, t1f_ref, tt1f_ref,
               q3b_ref, k3b_ref, v3b_ref, t3b_ref, q1b_ref, k1b_ref, t1b_ref, tt1b_ref,
               of_ref, ob_ref, state_ref, m_ref, attn_ref, x_ref):
    step = pl.program_id(0)

    @pl.when(step == 0)
    def _():
        state_ref[...] = jnp.zeros_like(state_ref)
        m_ref[...] = jnp.zeros_like(m_ref)
        attn_ref[...] = jnp.zeros_like(attn_ref)
        x_ref[...] = jnp.zeros_like(x_ref)

    wr = step % 2
    rd = 1 - wr
    c = DN_CHUNK
    nh = DN_HEADS
    hd_w = DN_HEAD_DIM
    b = DN_BASE
    npairs = nh
    ri = lax.broadcasted_iota(jnp.int32, (c, 2 * c), 0)
    lane2 = lax.broadcasted_iota(jnp.int32, (c, 2 * c), 1)
    ci = lane2 % c
    left = lane2 < c

    def pair_cols(tab, lane_a):
        return jnp.where(left, tab[:, lane_a:lane_a + 1], tab[:, lane_a + 1:lane_a + 2])

    def pair_slice(p):
        hd = (p % (nh // 2)) * 2
        return slice(hd * hd_w, (hd + 2) * hd_w)

    def table_lanes(p):
        lb = (p // (nh // 2)) * nh + (p % (nh // 2)) * 2
        return lb, 2 * nh + lb

    def prepare():
        ms = []
        for direction, (q_ref, k_ref, t_ref, tt_ref) in enumerate(
                ((q1f_ref, k1f_ref, t1f_ref, tt1f_ref), (q1b_ref, k1b_ref, t1b_ref, tt1b_ref))):
            upper = direction == 1
            incl = (ri <= ci) if upper else (ri >= ci)
            strict = (ri < ci) if upper else (ri > ci)
            tab = t_ref[...]
            tabt = tt_ref[...]
            for p in range(direction * nh // 2, (direction + 1) * nh // 2):
                cs = pair_slice(p)
                lb, lg = table_lanes(p)
                kp = k_ref[:, cs]
                beta = pair_cols(tab, lb)
                gcol = pair_cols(tab, lg)
                grow = jnp.concatenate([tabt[lg:lg + 1, :], tabt[lg + 1:lg + 2, :]], axis=1)
                decay = jnp.exp(jnp.where(incl, gcol - grow, NEG_INF))
                prod = _dot_nt(jnp.concatenate([q_ref[:, cs], kp], axis=0), _pair_block_diag(kp))
                m = jnp.where(strict, prod[c:] * beta * decay, 0.0)
                m_ref[wr, p] = m.astype(_BF)
                attn_ref[wr, p] = (prod[:c] * decay).astype(_BF)
                ms.append(m)
                yield
        n = 2 * npairs
        dg_all = jnp.concatenate([_diag_blocks_by_lane(m[:, h * c:(h + 1) * c]) for m in ms for h in range(2)], axis=0)
        half = n * b // 2
        dg_lo, dg_up = dg_all[:half], dg_all[half:]
        blk_half = lax.broadcasted_iota(jnp.int32, (half, c), 1) // b

        def column(dg, j):
            return jnp.take_along_axis(dg, blk_half * b + j, axis=1).reshape(n // 2, b, c)

        lane = lax.broadcasted_iota(jnp.int32, (b, c), 1)
        sub = lax.broadcasted_iota(jnp.int32, (b, c), 0)
        eye = jnp.broadcast_to(jnp.where(sub == lane % b, 1.0, 0.0), (n // 2, b, c))
        lo, up = eye, eye
        for t in range(b - 1):
            lo = lo - column(dg_lo, t) * lo[:, t:t + 1, :]
            up = up - column(dg_up, b - 1 - t) * up[:, b - 1 - t:b - t, :]
            if t % 3 == 2:
                yield
        for p in range(npairs):
            acc = lo if p < npairs // 2 else up
            r0 = 2 * (p % (npairs // 2))
            x_ref[wr, p] = jnp.concatenate(
                [_blocks_to_diagonal(acc[r0]), _blocks_to_diagonal(acc[r0 + 1])], axis=1).astype(_BF)
        yield

    def recur(group):
        ms = {p: m_ref[rd, p] for p in group}
        xs = {p: x_ref[rd, p] for p in group}
        size = b
        while size < c:
            off_diag = (ri // (2 * size) == ci // (2 * size)) & (ri // size != ci // size)
            zero = jnp.zeros_like(ms[group[0]])
            ts = {p: _dot(xs[p], _pair_block_diag(jnp.where(off_diag, ms[p], zero))).astype(_BF) for p in group}
            yield
            xs = {p: (xs[p].astype(_F32) - _dot(ts[p], _pair_block_diag(xs[p]))).astype(_BF) for p in group}
            yield
            size *= 2
        us, wqs, kdecs, gls = {}, {}, {}, {}
        for p in group:
            q_ref, k_ref, v_ref, t_ref = ((q3f_ref, k3f_ref, v3f_ref, t3f_ref), (q3b_ref, k3b_ref, v3b_ref, t3b_ref))[
                p // (npairs // 2)]
            cs = pair_slice(p)
            lb, lg = table_lanes(p)
            tab = t_ref[...]
            beta = pair_cols(tab, lb)
            gcol = pair_cols(tab, lg)
            q, k, v = (r[:, cs].astype(_F32) for r in (q_ref, k_ref, v_ref))
            glast = gcol[0:1, :] if p >= npairs // 2 else gcol[c - 1:c, :]
            eg = jnp.exp(gcol)
            vbeta = (v * beta).astype(_BF)
            kbeta = (k * (beta * eg)).astype(_BF)
            xb = xs[p]
            sols = [_dot(xb[:, h * c:(h + 1) * c],
                         jnp.concatenate([vbeta[:, h * hd_w:(h + 1) * hd_w], kbeta[:, h * hd_w:(h + 1) * hd_w]], axis=1))
                    for h in range(2)]
            us[p] = jnp.concatenate([sols[0][:, :hd_w], sols[1][:, :hd_w]], axis=1)
            w = jnp.concatenate([sols[0][:, hd_w:], sols[1][:, hd_w:]], axis=1)
            wqs[p] = jnp.concatenate([w, q * eg], axis=0).astype(_BF)
            kdecs[p] = (k * jnp.exp(glast - gcol)).astype(_BF)
            gls[p] = jnp.exp(glast)
            yield
        states = {p: state_ref[p] for p in group}
        wss = {p: _dot(wqs[p], _pair_block_diag(states[p].astype(_BF))) for p in group}
        yield
        vbs = {p: (us[p] - wss[p][:c]).astype(_BF) for p in group}
        for p in group:
            o_ref = of_ref if p < npairs // 2 else ob_ref
            o_ref[:, pair_slice(p)] = wss[p][c:] + _dot(attn_ref[rd, p], _pair_block_diag(vbs[p]))
        yield
        for p in group:
            upd = [_dot_tn(kdecs[p][:, h * hd_w:(h + 1) * hd_w], vbs[p][:, h * hd_w:(h + 1) * hd_w]) for h in range(2)]
            state_ref[p] = states[p] * gls[p] + jnp.concatenate(upd, axis=1)
        yield

    groups = [list(range(g, g + DN_GROUP)) for g in range(0, npairs, DN_GROUP)]
    for _ in itertools.zip_longest(itertools.chain(*(recur(g) for g in groups)), prepare()):
        pass


def _gated_delta(qd, kd, vd, tab, tabt):
    s = qd.shape[0]
    c = DN_CHUNK
    n = s // c
    npairs = DN_HEADS
    def chunk_of(stream, flip):
        def index(i):
            j = jnp.maximum(i - 1, 0) if stream == "rec" else jnp.minimum(i, n - 1)
            return n - 1 - j if flip else j
        return index

    def wide(stream, flip):
        index = chunk_of(stream, flip)
        return pl.BlockSpec((c, DN_WIDTH), lambda i: (index(i), 0))

    def table(stream, flip):
        index = chunk_of(stream, flip)
        return pl.BlockSpec((c, LANES), lambda i: (index(i), 0))

    def table_t(flip):
        index = chunk_of("prep", flip)
        return pl.BlockSpec((4 * DN_HEADS, c), lambda i: (0, index(i)))

    specs = []
    for flip in (False, True):
        specs += [wide("rec", flip), wide("rec", flip), wide("rec", flip), table("rec", flip),
                  wide("prep", flip), wide("prep", flip), table("prep", flip), table_t(flip)]
    return pl.pallas_call(
        _dn_kernel,
        grid=(n + 1,),
        in_specs=specs,
        out_specs=[wide("rec", False), wide("rec", True)],
        out_shape=[jax.ShapeDtypeStruct((s, DN_WIDTH), _F32)] * 2,
        scratch_shapes=[pltpu.VMEM((npairs, DN_HEAD_DIM, 2 * DN_HEAD_DIM), _F32),
                        pltpu.VMEM((2, npairs, c, 2 * c), _BF),
                        pltpu.VMEM((2, npairs, c, 2 * c), _BF),
                        pltpu.VMEM((2, npairs, c, 2 * c), _BF)],
        compiler_params=_cparams("arbitrary"),
    )(qd, kd, vd, tab, qd, kd, tab, tabt, qd, kd, vd, tab, qd, kd, tab, tabt)


def _token_order(blk_ref, stage_ref, dil, row0, rows):
    gw = ATT_GROUP_WIDTH
    src = pl.ds(row0 // dil, rows // dil)
    if dil == 1:
        return blk_ref[src, :].astype(_F32)
    for r in range(dil):
        for c in range(gw // LANES):
            stage_ref[c, pl.ds(row0 + r, rows // dil, stride=dil), :] = (
                blk_ref[src, r * gw + c * LANES:r * gw + (c + 1) * LANES].astype(_F32))
    return jnp.concatenate([stage_ref[c, pl.ds(row0, rows), :] for c in range(gw // LANES)], axis=1)


def _out_kernel(x_ref, o0_ref, o1_ref, o2_ref, l0_ref, l1_ref, l2_ref, of_ref, ob_ref, z_ref, gate_ref, vec_ref,
                wa_ref, wd_ref, wo_ref, out_ref, oatt_ref, odn_ref, y_ref, stage_ref, *, tm):
    d = D_MODEL
    dils = [dil for _, dil in ATT_GROUPS]
    rows = tm // OUT_ROW_SPLIT
    subs = [(r * rows, pl.ds(r * rows, rows)) for r in range(OUT_ROW_SPLIT)]
    for row0, rs in subs:
        o0, o1, o2 = (_token_order(ref, stage_ref.at[n], dils[n], row0, rows)
                      for n, ref in enumerate((o0_ref, o1_ref, o2_ref)))
        l0, l1, l2 = (_token_order(ref, stage_ref.at[3 + n], dils[n], row0, rows)
                      for n, ref in enumerate((l0_ref, l1_ref, l2_ref)))
        mx = jnp.maximum(jnp.maximum(l0, l1), l2)
        e0, e1, e2 = jnp.exp(l0 - mx), jnp.exp(l1 - mx), jnp.exp(l2 - mx)
        oatt_ref[rs, :] = ((e0 * o0 + e1 * o1 + e2 * o2) / (e0 + e1 + e2)).astype(_BF)
        for hd in range(DN_HEADS):
            sl = slice(hd * DN_HEAD_DIM, (hd + 1) * DN_HEAD_DIM)
            o = of_ref[rs, sl] + ob_ref[rs, sl]
            o = _rms(o) * vec_ref[5:6, 0:DN_HEAD_DIM]
            odn_ref[rs, sl] = (o * z_ref[rs, sl].astype(_F32)).astype(_BF)
    for row0, rs in subs:
        ya = _dot(oatt_ref[rs, :], wa_ref[...])
        yd = _dot(odn_ref[rs, :], wd_ref[...])
        y_ref[rs, :] = (gate_ref[rs, :d].astype(_F32) * ya + gate_ref[rs, d:].astype(_F32) * yd).astype(_BF)
    for row0, rs in subs:
        y = _dot(y_ref[rs, :], wo_ref[...])
        out_ref[rs, :] = x_ref[rs, :] + vec_ref[3:4, :] * (_rms(y) * vec_ref[4:5, :])


def _mixer_output(x, att, o_f, o_b, z, gates, vec, w_a, w_d, w_o):
    s, d = x.shape
    tm = min(ROW_TILE, s)
    row = lambda i: (i, 0)
    (o0, l0), (o1, l1), (o2, l2) = att
    gw = ATT_GROUP_WIDTH
    by_class = [pl.BlockSpec((tm // dil, dil * gw), row) for _, dil in ATT_GROUPS]
    return pl.pallas_call(
        functools.partial(_out_kernel, tm=tm),
        grid=(s // tm,),
        in_specs=[pl.BlockSpec((tm, d), row)] + by_class * 2
        + [pl.BlockSpec((tm, DN_WIDTH), row)] * 3 + [pl.BlockSpec((tm, 2 * d), row)]
        + [_resident(vec.shape), _resident(w_a.shape), _resident(w_d.shape), _resident(w_o.shape)],
        out_specs=pl.BlockSpec((tm, d), row),
        out_shape=jax.ShapeDtypeStruct((s, d), _F32),
        scratch_shapes=[pltpu.VMEM((tm, gw), _BF), pltpu.VMEM((tm, DN_WIDTH), _BF), pltpu.VMEM((tm, d), _BF),
                        pltpu.VMEM((6, gw // LANES, tm, LANES), _F32)],
        compiler_params=_cparams("arbitrary"),
    )(x, o0, o1, o2, l0, l1, l2, o_f, o_b, z, gates, vec, w_a, w_d, w_o)


def _vec(rows, d):
    pad = [jnp.zeros((d,), _F32)] * (SUBLANES - len(rows))
    return jnp.stack(list(rows) + pad, axis=0)


def kernel(x, c, positions, w_ada, b_ada, norm_pre, norm_post, ffn_w_in, ffn_w_out, w_in, conv_w, a_log, dt_bias,
           dn_norm_w, w_branch_att, w_branch_dn, w_out):
    b, s, d = x.shape
    assert b == 1 and d == D_MODEL and s % (DN_CHUNK * 16) == 0
    depth = w_ada.shape[0]
    x2 = x.reshape(s, d)
    mod = _ada_modulation(c, w_ada, b_ada)
    cos_t, sin_t = _rope_tables(positions)

    o_dn = 3 * ATT_WIDTH
    o_z = o_dn + 3 * DN_WIDTH
    o_ba = o_z + DN_WIDTH
    o_g = o_ba + 4 * DN_HEADS
    nh = DN_HEADS
    zeros_row = jnp.zeros((LANES,), _F32)

    for l in range(depth):
        def vec_for(sub, extra=None):
            rows = [norm_pre[l, sub], mod[l, sub, 0], mod[l, sub, 1], mod[l, sub, 2], norm_post[l, sub]]
            if extra is not None:
                rows.append(extra)
            return _vec(rows, d)

        x2 = _ffn_sublayer(x2, vec_for(0), ffn_w_in[l, 0].astype(_BF), ffn_w_out[l, 0].astype(_BF))

        wl = w_in[l]
        w_ba = jnp.pad(wl[:, o_ba:o_g], ((0, 0), (0, LANES - 4 * nh))).astype(_BF)
        wts = (wl[:, :o_dn].astype(_BF), wl[:, o_dn:o_z].astype(_BF), wl[:, o_z:o_ba].astype(_BF),
               wl[:, o_g:].astype(_BF), w_ba)
        neg_a = zeros_row.at[2 * nh:4 * nh].set(-jnp.exp(a_log[l].astype(_F32)).reshape(-1))
        dtb = zeros_row.at[2 * nh:4 * nh].set(dt_bias[l].astype(_F32).reshape(-1))
        dec = _vec([neg_a, dtb], LANES)
        qa, ka, va, (qd, kd, vd, z, gates, tab, tabt) = _mixer_projection(
            x2, vec_for(1), cos_t, sin_t, wts, conv_w[l], dec)
        att = [_dilated_attention(qa[g], ka[g], va[g], dil) for g, (_, dil) in enumerate(ATT_GROUPS)]
        o_f, o_b = _gated_delta(qd, kd, vd, tab, tabt)
        x2 = _mixer_output(x2, att, o_f, o_b, z, gates, vec_for(1, jnp.tile(dn_norm_w[l], d // DN_HEAD_DIM)),
                           w_branch_att[l].astype(_BF), w_branch_dn[l].astype(_BF), w_out[l].astype(_BF))

        x2 = _ffn_sublayer(x2, vec_for(2), ffn_w_in[l, 1].astype(_BF), ffn_w_out[l, 1].astype(_BF))
    return x2.reshape(b, s, d)
```

```python
import functools
import itertools

import jax
import jax.numpy as jnp
from jax import lax
from jax.experimental import pallas as pl
from jax.experimental.pallas import tpu as pltpu

D_MODEL = 1024
DEPTH = 2
N_SUBLAYERS = 3
ATT_GROUPS = ((128, 1), (512, 4), (2048, 16))
ATT_HEADS_PER_GROUP = 4
ATT_HEAD_DIM = 64
ATT_WIDTH = len(ATT_GROUPS) * ATT_HEADS_PER_GROUP * ATT_HEAD_DIM
ATT_GROUP_WIDTH = ATT_HEADS_PER_GROUP * ATT_HEAD_DIM
ATT_RADIUS = 64
ROPE_THETA = 10000.0
DN_HEADS = 6
DN_HEAD_DIM = 128
DN_WIDTH = DN_HEADS * DN_HEAD_DIM
DN_CONV = 5
FFN_DIM = 2816
EPS = 1e-6
NEG_INF = -1e30

LANES = 128
SUBLANES = 8
BF16_ROWS = 16
MXU_DIM = 256
VMEM_LIMIT_BYTES = 56 * 1024 * 1024

ROW_TILE = 512
FFN_CHUNKS = 2
FFN_ROW_TILE = 1024
FFN_ROW_SPLIT = 2
OUT_ROW_SPLIT = 2
PROJ_CHUNK = 256
ATT_Q_TILE = 1024
ATT_Q_SUB = 128
DN_CHUNK = 128
DN_BASE = 16
DN_GROUP = 6
HALO = BF16_ROWS

_BF = jnp.bfloat16
_F32 = jnp.float32


def _cparams(*sem):
    return pltpu.CompilerParams(dimension_semantics=sem, vmem_limit_bytes=VMEM_LIMIT_BYTES)


def _resident(shape):
    nd = len(shape)
    return pl.BlockSpec(shape, lambda *_: (0,) * nd, pipeline_mode=pl.Buffered(1))


def _rms(t):
    return t * lax.rsqrt(jnp.mean(t * t, axis=-1, keepdims=True) + EPS)


def _dot(a, b):
    return jnp.dot(a, b, preferred_element_type=_F32)


def _dot_nt(a, b):
    return lax.dot_general(a, b, (((1,), (1,)), ((), ())), preferred_element_type=_F32)


def _dot_tn(a, b):
    return lax.dot_general(a, b, (((0,), (0,)), ((), ())), preferred_element_type=_F32)


def _ada_kernel(c_ref, w_ref, b_ref, o_ref):
    c = c_ref[...]
    s = c * jax.nn.sigmoid(c)
    o_ref[0] = jnp.sum(w_ref[0] * s, axis=0, keepdims=True) + b_ref[0]


def _ada_modulation(c, w_ada, b_ada):
    depth, d, n = w_ada.shape
    tn = 1024
    out = pl.pallas_call(
        _ada_kernel,
        grid=(depth, n // tn),
        in_specs=[
            pl.BlockSpec((d, 1), lambda l, j: (0, 0)),
            pl.BlockSpec((1, d, tn), lambda l, j: (l, 0, j)),
            pl.BlockSpec((1, 1, tn), lambda l, j: (l, 0, j)),
        ],
        out_specs=pl.BlockSpec((1, 1, tn), lambda l, j: (l, 0, j)),
        out_shape=jax.ShapeDtypeStruct((depth, 1, n), _F32),
        compiler_params=_cparams("arbitrary", "arbitrary"),
    )(c.reshape(d, 1), w_ada, b_ada.reshape(depth, 1, n))
    return out.reshape(depth, N_SUBLAYERS, 3, d)


def _rope_kernel(pos_ref, inv_ref, cos_ref, sin_ref):
    ang = pos_ref[...].astype(_F32) * inv_ref[...]
    lane = lax.broadcasted_iota(jnp.int32, ang.shape, 1)
    first_half = (lane % ATT_HEAD_DIM) < (ATT_HEAD_DIM // 2)
    cos_ref[...] = jnp.cos(ang)
    sin_ref[...] = jnp.where(first_half, -jnp.sin(ang), jnp.sin(ang))


def _rope_tables(positions):
    s = positions.shape[1]
    ts = min(2048, s)
    half = ATT_HEAD_DIM // 2
    inv = ROPE_THETA ** (-jnp.arange(half, dtype=_F32) * 2.0 / ATT_HEAD_DIM)
    inv_row = jnp.tile(inv, LANES // half).reshape(1, LANES)
    return pl.pallas_call(
        _rope_kernel,
        grid=(s // ts,),
        in_specs=[pl.BlockSpec((ts, 1), lambda i: (i, 0)), pl.BlockSpec((1, LANES), lambda i: (0, 0))],
        out_specs=[pl.BlockSpec((ts, LANES), lambda i: (i, 0))] * 2,
        out_shape=[jax.ShapeDtypeStruct((s, LANES), _F32)] * 2,
        compiler_params=_cparams("arbitrary"),
    )(positions.reshape(s, 1), inv_row)


def _ada_pre(x, vec_ref):
    a = vec_ref[0:1, :] * (1.0 + vec_ref[2:3, :])
    return _rms(x) * a + vec_ref[1:2, :]


def _ffn_kernel(x_ref, vec_ref, win_ref, wout_ref, o_ref, h_ref, act_ref, *, tm):
    rows = [pl.ds(r * (tm // FFN_ROW_SPLIT), tm // FFN_ROW_SPLIT) for r in range(FFN_ROW_SPLIT)]
    tiles = FFN_DIM // MXU_DIM
    bounds = [MXU_DIM * (j * tiles // FFN_CHUNKS) for j in range(FFN_CHUNKS)] + [FFN_DIM]
    for r in rows:
        h_ref[r, :] = _ada_pre(x_ref[r, :], vec_ref).astype(_BF)
    for lo, hi in zip(bounds[:-1], bounds[1:]):
        for r in rows:
            g = _dot(h_ref[r, :], win_ref[:, lo:hi])
            u = _dot(h_ref[r, :], win_ref[:, FFN_DIM + lo:FFN_DIM + hi])
            act_ref[r, lo:hi] = (g * jax.nn.sigmoid(g) * u).astype(_BF)
    for r in rows:
        y = _dot(act_ref[r, :], wout_ref[...])
        o_ref[r, :] = x_ref[r, :] + (0.5 * vec_ref[3:4, :]) * (_rms(y) * vec_ref[4:5, :])


def _ffn_sublayer(x, vec, w_in, w_out):
    s, d = x.shape
    tm = min(FFN_ROW_TILE, s)
    return pl.pallas_call(
        functools.partial(_ffn_kernel, tm=tm),
        grid=(s // tm,),
        in_specs=[
            pl.BlockSpec((tm, d), lambda i: (i, 0)),
            _resident(vec.shape),
            _resident(w_in.shape),
            _resident(w_out.shape),
        ],
        out_specs=pl.BlockSpec((tm, d), lambda i: (i, 0)),
        out_shape=jax.ShapeDtypeStruct((s, d), _F32),
        scratch_shapes=[pltpu.VMEM((tm, d), _BF), pltpu.VMEM((tm, FFN_DIM), _BF)],
        compiler_params=_cparams("arbitrary"),
    )(x, vec, w_in, w_out)


def _swap_halves(t):
    n = t.shape[1]
    half = ATT_HEAD_DIM // 2
    from_right = pltpu.roll(t, n - half, 1)
    from_left = pltpu.roll(t, half, 1)
    lane = lax.broadcasted_iota(jnp.int32, t.shape, 1)
    return jnp.where((lane % ATT_HEAD_DIM) < half, from_right, from_left)


def _chunk_cumsums(g, tm):
    row = lax.broadcasted_iota(jnp.int32, g.shape, 0) % DN_CHUNK
    fwd, bwd = g, g
    sh = 1
    while sh < DN_CHUNK:
        fwd = fwd + jnp.where(row >= sh, pltpu.roll(fwd, sh, 0), 0.0)
        bwd = bwd + jnp.where(row < DN_CHUNK - sh, pltpu.roll(bwd, tm - sh, 0), 0.0)
        sh *= 2
    return fwd, bwd


def _store_by_class(val, out_ref, stage_ref, dil, tm):
    gw = ATT_GROUP_WIDTH
    if dil == 1:
        out_ref[...] = val.astype(_BF)
        return
    for c in range(gw // LANES):
        stage_ref[c] = val[:, c * LANES:(c + 1) * LANES]
    for r in range(dil):
        for c in range(gw // LANES):
            out_ref[:, r * gw + c * LANES:r * gw + (c + 1) * LANES] = (
                stage_ref[c, pl.ds(r, tm // dil, stride=dil), :].astype(_BF))


def _proj_kernel(xp_ref, x_ref, xn_ref, vec_ref, cos_ref, sin_ref, watt_ref, wdn_ref, wz_ref, wg_ref, wba_ref,
                 conv_ref, dec_ref,
                 q0_ref, q1_ref, q2_ref, k0_ref, k1_ref, k2_ref, v0_ref, v1_ref, v2_ref,
                 qd_ref, kd_ref, vd_ref, z_ref, gate_ref, tab_ref, tabt_ref,
                 h_ref, pdn_ref, stage_ref, *, tm):
    i = pl.program_id(0)
    last = pl.num_programs(0) - 1
    cw = PROJ_CHUNK
    gw = ATT_GROUP_WIDTH
    x_ext = jnp.concatenate([xp_ref[0], x_ref[...], xn_ref[0]], axis=0)
    h_ref[...] = _ada_pre(x_ext, vec_ref).astype(_BF)
    inner = pl.ds(HALO, tm)
    cos = jnp.tile(cos_ref[...], (1, cw // LANES))
    sin = jnp.tile(sin_ref[...], (1, cw // LANES))
    row = lax.broadcasted_iota(jnp.int32, (tm + 2 * HALO, 1), 0)
    outside = ((i == 0) & (row < HALO)) | ((i == last) & (row >= HALO + tm))
    pad = DN_CONV // 2
    att_out = ((q0_ref, q1_ref, q2_ref), (k0_ref, k1_ref, k2_ref), (v0_ref, v1_ref, v2_ref))
    dn_out = (qd_ref, kd_ref, vd_ref)

    def att_task(kind, g):
        col = kind * ATT_WIDTH + g * gw

        def epilogue(t):
            if kind < 2:
                t = t * cos + _swap_halves(t) * sin
            if kind == 0:
                t = t * (ATT_HEAD_DIM ** -0.5)
            _store_by_class(t, att_out[kind][g], stage_ref.at[kind], ATT_GROUPS[g][1], tm)
        return (lambda: _dot(h_ref[inner, :], watt_ref[:, col:col + cw])), epilogue

    def dn_task(c):
        kind, col = divmod(c * cw, DN_WIDTH)
        slot = c % 2

        def epilogue(t):
            p = jnp.where(outside, 0.0, t)
            wc = conv_ref[:, c * cw:(c + 1) * cw]
            acc = p[HALO:HALO + tm] * wc[pad:pad + 1, :]
            for tap in range(DN_CONV):
                if tap != pad:
                    shifted = pltpu.roll(p, (pad - tap) % (tm + 2 * HALO), 0)
                    acc = acc + shifted[HALO:HALO + tm] * wc[tap:tap + 1, :]
            dn = acc * jax.nn.sigmoid(acc)
            for hd in range(cw // DN_HEAD_DIM):
                t_h = dn[:, hd * DN_HEAD_DIM:(hd + 1) * DN_HEAD_DIM]
                if kind < 2:
                    t_h = t_h * lax.rsqrt(jnp.sum(t_h * t_h, axis=-1, keepdims=True) + EPS)
                if kind == 0:
                    t_h = t_h * (DN_HEAD_DIM ** -0.5)
                dn_out[kind][:, col + hd * DN_HEAD_DIM:col + (hd + 1) * DN_HEAD_DIM] = t_h.astype(_BF)
        return (lambda: _dot(h_ref[...], wdn_ref[:, c * cw:(c + 1) * cw])), epilogue

    def z_task(c):
        def epilogue(t):
            z_ref[:, c * cw:(c + 1) * cw] = (t * jax.nn.sigmoid(t)).astype(_BF)
        return (lambda: _dot(h_ref[inner, :], wz_ref[:, c * cw:(c + 1) * cw])), epilogue

    def gate_task(c):
        def epilogue(t):
            gate_ref[:, c * cw:(c + 1) * cw] = jax.nn.sigmoid(t).astype(_BF)
        return (lambda: _dot(h_ref[inner, :], wg_ref[:, c * cw:(c + 1) * cw])), epilogue

    def table_task():
        def epilogue(raw):
            lane = lax.broadcasted_iota(jnp.int32, raw.shape, 1)
            beta = jax.nn.sigmoid(raw)
            log_decay = dec_ref[0:1, :] * jax.nn.softplus(raw + dec_ref[1:2, :])
            cum_f, cum_b = _chunk_cumsums(log_decay, tm)
            nh = DN_HEADS
            tab = jnp.where(lane < 2 * nh, beta, jnp.where(lane < 3 * nh, cum_f, jnp.where(lane < 4 * nh, cum_b, 0.0)))
            tab_ref[...] = tab
            tabt_ref[...] = tab.T[:4 * nh, :]
        return (lambda: _dot(h_ref[inner, :], wba_ref[...])), epilogue

    light = ([att_task(kind, g) for kind in range(3) for g in range(len(ATT_GROUPS))]
             + [z_task(c) for c in range(DN_WIDTH // cw)] + [gate_task(c) for c in range(2 * D_MODEL // cw)]
             + [table_task()])
    heavy = [dn_task(c) for c in range(3 * DN_WIDTH // cw)]
    tasks = []
    for n, task in enumerate(heavy):
        tasks += [task] + light[n * len(light) // len(heavy):(n + 1) * len(light) // len(heavy)]
    pending = None
    for matmul, epilogue in tasks:
        val = matmul()
        if pending is not None:
            pending[1](pending[0])
        pending = (val, epilogue)
    pending[1](pending[0])


def _mixer_projection(x, vec, cos_t, sin_t, wts, conv_w, dec):
    s, d = x.shape
    tm = min(ROW_TILE, s)
    nt = s // tm
    hb = tm // HALO
    x3 = x.reshape(s // HALO, HALO, d)
    row = lambda i: (i, 0)
    w_att, w_dn, w_z, w_g, w_ba = wts
    bf = lambda n: jax.ShapeDtypeStruct((s, n), _BF)
    gw = ATT_GROUP_WIDTH
    att_specs = [pl.BlockSpec((tm // dil, dil * gw), row) for _, dil in ATT_GROUPS] * 3
    att_shapes = [jax.ShapeDtypeStruct((s // dil, dil * gw), _BF) for _, dil in ATT_GROUPS] * 3
    outs = pl.pallas_call(
        functools.partial(_proj_kernel, tm=tm),
        grid=(nt,),
        in_specs=[
            pl.BlockSpec((1, HALO, d), lambda i: (jnp.maximum(i * hb - 1, 0), 0, 0)),
            pl.BlockSpec((tm, d), row),
            pl.BlockSpec((1, HALO, d), lambda i: (jnp.minimum((i + 1) * hb, s // HALO - 1), 0, 0)),
            _resident(vec.shape),
            pl.BlockSpec((tm, LANES), row),
            pl.BlockSpec((tm, LANES), row),
            _resident(w_att.shape), _resident(w_dn.shape), _resident(w_z.shape), _resident(w_g.shape),
            _resident(w_ba.shape), _resident(conv_w.shape), _resident(dec.shape),
        ],
        out_specs=att_specs + [
            pl.BlockSpec((tm, DN_WIDTH), row), pl.BlockSpec((tm, DN_WIDTH), row), pl.BlockSpec((tm, DN_WIDTH), row),
            pl.BlockSpec((tm, DN_WIDTH), row), pl.BlockSpec((tm, 2 * d), row),
            pl.BlockSpec((tm, LANES), row), pl.BlockSpec((4 * DN_HEADS, tm), lambda i: (0, i)),
        ],
        out_shape=att_shapes + [bf(DN_WIDTH), bf(DN_WIDTH), bf(DN_WIDTH), bf(DN_WIDTH), bf(2 * d),
                                jax.ShapeDtypeStruct((s, LANES), _F32), jax.ShapeDtypeStruct((4 * DN_HEADS, s), _F32)],
        scratch_shapes=[pltpu.VMEM((tm + 2 * HALO, d), _BF),
                        pltpu.VMEM((2, tm + 2 * HALO, PROJ_CHUNK), _F32),
                        pltpu.VMEM((3, gw // LANES, tm, LANES), _F32)],
        compiler_params=_cparams("arbitrary"),
    )(x3, x, x3, vec, cos_t, sin_t, w_att, w_dn, w_z, w_g, w_ba, conv_w, dec)
    return outs[0:3], outs[3:6], outs[6:9], outs[9:]


def _att_kernel(q_ref, kp_ref, k_ref, kn_ref, vp_ref, v_ref, vn_ref, o_ref, lse_ref, *, nq, n_rows):
    i = pl.program_id(1)
    r = ATT_RADIUS
    qs = ATT_Q_SUB
    kw = qs + 2 * r
    nsub = nq // qs
    qi = lax.broadcasted_iota(jnp.int32, (2 * qs, kw), 0) % qs
    kj = lax.broadcasted_iota(jnp.int32, (2 * qs, kw), 1) - r
    band = jnp.where(jnp.abs(kj - qi) <= r, 0.0, NEG_INF)
    lane = lax.broadcasted_iota(jnp.int32, (1, LANES), 1)
    low = lane < ATT_HEAD_DIM
    kcol = lax.broadcasted_iota(jnp.int32, (1, kw), 1)

    def window(prev_ref, main_ref, next_ref, j, cs):
        lo, hi = j * qs - r, j * qs + qs + r
        parts = [prev_ref[:, cs]] if lo < 0 else []
        parts.append(main_ref[max(lo, 0):min(hi, nq), cs])
        if hi > nq:
            parts.append(next_ref[:, cs])
        return parts[0] if len(parts) == 1 else jnp.concatenate(parts, axis=0)

    for j in range(nsub):
        r0 = j * qs
        bias = band
        if j == 0 or j == nsub - 1:
            kidx = i * nq + r0 - r + kcol
            bias = band + jnp.where((kidx < 0) | (kidx >= n_rows), NEG_INF, 0.0)
        for hp in range(ATT_GROUP_WIDTH // LANES):
            cs = slice(hp * LANES, (hp + 1) * LANES)
            qp = q_ref[r0:r0 + qs, cs]
            kp = window(kp_ref, k_ref, kn_ref, j, cs)
            vp = window(vp_ref, v_ref, vn_ref, j, cs)
            zero = jnp.zeros_like(qp)
            qq = jnp.concatenate([jnp.where(low, qp, zero), jnp.where(low, zero, qp)], axis=0)
            sc = _dot_nt(qq, kp) + bias
            m = jnp.max(sc, axis=-1, keepdims=True)
            p = jnp.exp(sc - m)
            den = jnp.sum(p, axis=-1, keepdims=True)
            pv = _dot(p.astype(_BF), vp) / den
            lse = m + jnp.log(den)
            o_ref[r0:r0 + qs, cs] = jnp.where(low, pv[:qs], pv[qs:]).astype(_BF)
            lse_ref[r0:r0 + qs, cs] = jnp.where(low, lse[:qs], lse[qs:])


def _dilated_attention(qv, kv, vv, dilation):
    n_rows = qv.shape[0]
    nq = min(ATT_Q_TILE, n_rows)
    nt = n_rows // nq
    r = ATT_RADIUS
    gw = ATT_GROUP_WIDTH
    main = pl.BlockSpec((nq, gw), lambda c, i: (i, c))
    prev = pl.BlockSpec((r, gw), lambda c, i: (jnp.maximum(i * (nq // r) - 1, 0), c))
    nxt = pl.BlockSpec((r, gw), lambda c, i: (jnp.minimum((i + 1) * (nq // r), n_rows // r - 1), c))
    out = main
    return pl.pallas_call(
        functools.partial(_att_kernel, nq=nq, n_rows=n_rows),
        grid=(dilation, nt),
        in_specs=[main, prev, main, nxt, prev, main, nxt],
        out_specs=[out, out],
        out_shape=[jax.ShapeDtypeStruct((n_rows, dilation * gw), _BF),
                   jax.ShapeDtypeStruct((n_rows, dilation * gw), _F32)],
        compiler_params=_cparams("arbitrary", "arbitrary"),
    )(qv, kv, kv, kv, vv, vv, vv)


def _diag_blocks_by_lane(m):
    c = m.shape[0]
    b = DN_BASE
    blk = lax.broadcasted_iota(jnp.int32, (b, c), 1) // b
    dg = jnp.zeros((b, c), _F32)
    for bi in range(c // b):
        dg = jnp.where(blk == bi, m[bi * b:(bi + 1) * b, :], dg)
    return dg


def _blocks_to_diagonal(acc):
    b, c = acc.shape
    blk = lax.broadcasted_iota(jnp.int32, (b, c), 1) // b
    return jnp.concatenate([jnp.where(blk == bi, acc, 0.0) for bi in range(c // b)], axis=0)


def _pair_block_diag(t):
    c = t.shape[0]
    left = lax.broadcasted_iota(jnp.int32, (1, 2 * c), 1) < c
    zero = jnp.zeros_like(t)
    return jnp.concatenate([jnp.where(left, t, zero), jnp.where(left, zero, t)], axis=0)


def _dn_kernel(qf_ref, kf_ref, vf_ref, tf_ref, ttf_ref, qb_ref, kb_ref, vb_ref, tb_ref, ttb_ref,
               of_ref, ob_ref, state_ref, m_ref, attn_ref, x_ref, vbeta_ref, kbeta_ref, qg_ref, kdec_ref, gl_ref):
    step = pl.program_id(0)
    slots = (m_ref, attn_ref, x_ref, vbeta_ref, kbeta_ref, qg_ref, kdec_ref, gl_ref)

    @pl.when(step == 0)
    def _():
        state_ref[...] = jnp.zeros_like(state_ref)
        for ref in slots:
            ref[...] = jnp.zeros_like(ref)

    wr = step % 2
    rd = 1 - wr
    c = DN_CHUNK
    nh = DN_HEADS
    hd_w = DN_HEAD_DIM
    b = DN_BASE
    npairs = nh
    ri = lax.broadcasted_iota(jnp.int32, (c, 2 * c), 0)
    lane2 = lax.broadcasted_iota(jnp.int32, (c, 2 * c), 1)
    ci = lane2 % c
    left = lane2 < c

    def pair_cols(tab, lane_a):
        return jnp.where(left, tab[:, lane_a:lane_a + 1], tab[:, lane_a + 1:lane_a + 2])

    def pair_slice(p):
        hd = (p % (nh // 2)) * 2
        return slice(hd * hd_w, (hd + 2) * hd_w)

    def table_lanes(p):
        lb = (p // (nh // 2)) * nh + (p % (nh // 2)) * 2
        return lb, 2 * nh + lb

    def prepare():
        ms = []
        for direction, (q_ref, k_ref, v_ref, t_ref, tt_ref) in enumerate(
                ((qf_ref, kf_ref, vf_ref, tf_ref, ttf_ref), (qb_ref, kb_ref, vb_ref, tb_ref, ttb_ref))):
            upper = direction == 1
            incl = (ri <= ci) if upper else (ri >= ci)
            strict = (ri < ci) if upper else (ri > ci)
            tab = t_ref[...]
            tabt = tt_ref[...]
            for p in range(direction * nh // 2, (direction + 1) * nh // 2):
                cs = pair_slice(p)
                lb, lg = table_lanes(p)
                kp = k_ref[:, cs]
                beta = pair_cols(tab, lb)
                gcol = pair_cols(tab, lg)
                grow = jnp.concatenate([tabt[lg:lg + 1, :], tabt[lg + 1:lg + 2, :]], axis=1)
                decay = jnp.exp(jnp.where(incl, gcol - grow, NEG_INF))
                prod = _dot_nt(jnp.concatenate([q_ref[:, cs], kp], axis=0), _pair_block_diag(kp))
                m = jnp.where(strict, prod[c:] * beta * decay, 0.0)
                m_ref[wr, p] = m.astype(_BF)
                attn_ref[wr, p] = (prod[:c] * decay).astype(_BF)
                ms.append(m)
                yield
                k = kp.astype(_F32)
                glast = gcol[0:1, :] if upper else gcol[c - 1:c, :]
                eg = jnp.exp(gcol)
                vbeta_ref[wr, p] = (v_ref[:, cs].astype(_F32) * beta).astype(_BF)
                kbeta_ref[wr, p] = (k * (beta * eg)).astype(_BF)
                qg_ref[wr, p] = (q_ref[:, cs].astype(_F32) * eg).astype(_BF)
                kdec_ref[wr, p] = (k * jnp.exp(glast - gcol)).astype(_BF)
                gl_ref[wr, p] = jnp.broadcast_to(jnp.exp(glast), (SUBLANES, 2 * c))
                yield
        n = 2 * npairs
        dg_all = jnp.concatenate([_diag_blocks_by_lane(m[:, h * c:(h + 1) * c]) for m in ms for h in range(2)], axis=0)
        half = n * b // 2
        dg_lo, dg_up = dg_all[:half], dg_all[half:]
        blk_half = lax.broadcasted_iota(jnp.int32, (half, c), 1) // b

        def column(dg, j):
            return jnp.take_along_axis(dg, blk_half * b + j, axis=1).reshape(n // 2, b, c)

        lane = lax.broadcasted_iota(jnp.int32, (b, c), 1)
        sub = lax.broadcasted_iota(jnp.int32, (b, c), 0)
        eye = jnp.broadcast_to(jnp.where(sub == lane % b, 1.0, 0.0), (n // 2, b, c))
        lo, up = eye, eye
        for t in range(b - 1):
            lo = lo - column(dg_lo, t) * lo[:, t:t + 1, :]
            up = up - column(dg_up, b - 1 - t) * up[:, b - 1 - t:b - t, :]
            if t % 3 == 2:
                yield
        for p in range(npairs):
            acc = lo if p < npairs // 2 else up
            r0 = 2 * (p % (npairs // 2))
            x_ref[wr, p] = jnp.concatenate(
                [_blocks_to_diagonal(acc[r0]), _blocks_to_diagonal(acc[r0 + 1])], axis=1).astype(_BF)
        yield

    def recur(group):
        ms = {p: m_ref[rd, p] for p in group}
        xs = {p: x_ref[rd, p] for p in group}
        size = b
        one = jnp.ones((), _BF)
        while size < c:
            off_diag = (ri // (2 * size) == ci // (2 * size)) & (ri // size != ci // size)
            zero = jnp.zeros_like(ms[group[0]])
            ts = {p: _dot(xs[p], _pair_block_diag(jnp.where(off_diag, ms[p], zero))).astype(_BF) for p in group}
            yield
            xs = {p: _dot(jnp.where(ri == ci, one, -ts[p]), _pair_block_diag(xs[p])).astype(_BF) for p in group}
            yield
            size *= 2
        us, wqs = {}, {}
        for p in group:
            vbeta, kbeta = vbeta_ref[rd, p], kbeta_ref[rd, p]
            sols = [_dot(xs[p][:, h * c:(h + 1) * c],
                         jnp.concatenate([vbeta[:, h * hd_w:(h + 1) * hd_w], kbeta[:, h * hd_w:(h + 1) * hd_w]], axis=1))
                    for h in range(2)]
            us[p] = jnp.concatenate([sols[0][:, :hd_w], sols[1][:, :hd_w]], axis=1)
            w = jnp.concatenate([sols[0][:, hd_w:], sols[1][:, hd_w:]], axis=1)
            wqs[p] = jnp.concatenate([w.astype(_BF), qg_ref[rd, p]], axis=0)
            yield
        states = {p: state_ref[p] for p in group}
        wss = {p: _dot(wqs[p], _pair_block_diag(states[p].astype(_BF))) for p in group}
        yield
        vbs = {p: (us[p] - wss[p][:c]).astype(_BF) for p in group}
        for p in group:
            o_ref = of_ref if p < npairs // 2 else ob_ref
            o_ref[:, pair_slice(p)] = wss[p][c:] + _dot(attn_ref[rd, p], _pair_block_diag(vbs[p]))
        yield
        for p in group:
            kdec = kdec_ref[rd, p]
            upd = [_dot_tn(kdec[:, h * hd_w:(h + 1) * hd_w], vbs[p][:, h * hd_w:(h + 1) * hd_w]) for h in range(2)]
            state_ref[p] = states[p] * gl_ref[rd, p, 0:1, :] + jnp.concatenate(upd, axis=1)
        yield

    groups = [list(range(g, g + DN_GROUP)) for g in range(0, npairs, DN_GROUP)]
    for _ in itertools.zip_longest(itertools.chain(*(recur(g) for g in groups)), prepare()):
        pass


def _gated_delta(qd, kd, vd, tab, tabt):
    s = qd.shape[0]
    c = DN_CHUNK
    n = s // c
    npairs = DN_HEADS
    def chunk_of(stream, flip):
        def index(i):
            j = jnp.maximum(i - 1, 0) if stream == "rec" else jnp.minimum(i, n - 1)
            return n - 1 - j if flip else j
        return index

    def wide(stream, flip):
        index = chunk_of(stream, flip)
        return pl.BlockSpec((c, DN_WIDTH), lambda i: (index(i), 0))

    def table(flip):
        index = chunk_of("prep", flip)
        return pl.BlockSpec((c, LANES), lambda i: (index(i), 0))

    def table_t(flip):
        index = chunk_of("prep", flip)
        return pl.BlockSpec((4 * DN_HEADS, c), lambda i: (0, index(i)))

    specs = []
    for flip in (False, True):
        specs += [wide("prep", flip), wide("prep", flip), wide("prep", flip), table(flip), table_t(flip)]
    slot = pltpu.VMEM((2, npairs, c, 2 * c), _BF)
    return pl.pallas_call(
        _dn_kernel,
        grid=(n + 1,),
        in_specs=specs,
        out_specs=[wide("rec", False), wide("rec", True)],
        out_shape=[jax.ShapeDtypeStruct((s, DN_WIDTH), _F32)] * 2,
        scratch_shapes=[pltpu.VMEM((npairs, DN_HEAD_DIM, 2 * DN_HEAD_DIM), _F32),
                        slot, slot, slot,
                        slot, slot, slot, slot,
                        pltpu.VMEM((2, npairs, SUBLANES, 2 * c), _F32)],
        compiler_params=_cparams("arbitrary"),
    )(qd, kd, vd, tab, tabt, qd, kd, vd, tab, tabt)


def _token_order(blk_ref, stage_ref, dil, row0, rows):
    gw = ATT_GROUP_WIDTH
    src = pl.ds(row0 // dil, rows // dil)
    if dil == 1:
        return blk_ref[src, :].astype(_F32)
    for r in range(dil):
        for c in range(gw // LANES):
            stage_ref[c, pl.ds(row0 + r, rows // dil, stride=dil), :] = (
                blk_ref[src, r * gw + c * LANES:r * gw + (c + 1) * LANES].astype(_F32))
    return jnp.concatenate([stage_ref[c, pl.ds(row0, rows), :] for c in range(gw // LANES)], axis=1)


def _out_kernel(x_ref, o0_ref, o1_ref, o2_ref, l0_ref, l1_ref, l2_ref, of_ref, ob_ref, z_ref, gate_ref, vec_ref,
                wa_ref, wd_ref, wo_ref, out_ref, oatt_ref, odn_ref, y_ref, stage_ref, *, tm):
    d = D_MODEL
    dils = [dil for _, dil in ATT_GROUPS]
    rows = tm // OUT_ROW_SPLIT
    subs = [(r * rows, pl.ds(r * rows, rows)) for r in range(OUT_ROW_SPLIT)]
    for row0, rs in subs:
        o0, o1, o2 = (_token_order(ref, stage_ref.at[n], dils[n], row0, rows)
                      for n, ref in enumerate((o0_ref, o1_ref, o2_ref)))
        l0, l1, l2 = (_token_order(ref, stage_ref.at[3 + n], dils[n], row0, rows)
                      for n, ref in enumerate((l0_ref, l1_ref, l2_ref)))
        mx = jnp.maximum(jnp.maximum(l0, l1), l2)
        e0, e1, e2 = jnp.exp(l0 - mx), jnp.exp(l1 - mx), jnp.exp(l2 - mx)
        oatt_ref[rs, :] = ((e0 * o0 + e1 * o1 + e2 * o2) / (e0 + e1 + e2)).astype(_BF)
        for hd in range(DN_HEADS):
            sl = slice(hd * DN_HEAD_DIM, (hd + 1) * DN_HEAD_DIM)
            o = of_ref[rs, sl] + ob_ref[rs, sl]
            o = _rms(o) * vec_ref[5:6, 0:DN_HEAD_DIM]
            odn_ref[rs, sl] = (o * z_ref[rs, sl].astype(_F32)).astype(_BF)
    for row0, rs in subs:
        ya = _dot(oatt_ref[rs, :], wa_ref[...])
        yd = _dot(odn_ref[rs, :], wd_ref[...])
        y_ref[rs, :] = (gate_ref[rs, :d].astype(_F32) * ya + gate_ref[rs, d:].astype(_F32) * yd).astype(_BF)
    for row0, rs in subs:
        y = _dot(y_ref[rs, :], wo_ref[...])
        out_ref[rs, :] = x_ref[rs, :] + vec_ref[3:4, :] * (_rms(y) * vec_ref[4:5, :])


def _mixer_output(x, att, o_f, o_b, z, gates, vec, w_a, w_d, w_o):
    s, d = x.shape
    tm = min(ROW_TILE, s)
    row = lambda i: (i, 0)
    (o0, l0), (o1, l1), (o2, l2) = att
    gw = ATT_GROUP_WIDTH
    by_class = [pl.BlockSpec((tm // dil, dil * gw), row) for _, dil in ATT_GROUPS]
    return pl.pallas_call(
        functools.partial(_out_kernel, tm=tm),
        grid=(s // tm,),
        in_specs=[pl.BlockSpec((tm, d), row)] + by_class * 2
        + [pl.BlockSpec((tm, DN_WIDTH), row)] * 3 + [pl.BlockSpec((tm, 2 * d), row)]
        + [_resident(vec.shape), _resident(w_a.shape), _resident(w_d.shape), _resident(w_o.shape)],
        out_specs=pl.BlockSpec((tm, d), row),
        out_shape=jax.ShapeDtypeStruct((s, d), _F32),
        scratch_shapes=[pltpu.VMEM((tm, gw), _BF), pltpu.VMEM((tm, DN_WIDTH), _BF), pltpu.VMEM((tm, d), _BF),
                        pltpu.VMEM((6, gw // LANES, tm, LANES), _F32)],
        compiler_params=_cparams("arbitrary"),
    )(x, o0, o1, o2, l0, l1, l2, o_f, o_b, z, gates, vec, w_a, w_d, w_o)


def _vec(rows, d):
    pad = [jnp.zeros((d,), _F32)] * (SUBLANES - len(rows))
    return jnp.stack(list(rows) + pad, axis=0)


def kernel(x, c, positions, w_ada, b_ada, norm_pre, norm_post, ffn_w_in, ffn_w_out, w_in, conv_w, a_log, dt_bias,
           dn_norm_w, w_branch_att, w_branch_dn, w_out):
    b, s, d = x.shape
    assert b == 1 and d == D_MODEL and s % (DN_CHUNK * 16) == 0
    depth = w_ada.shape[0]
    x2 = x.reshape(s, d)
    mod = _ada_modulation(c, w_ada, b_ada)
    cos_t, sin_t = _rope_tables(positions)

    o_dn = 3 * ATT_WIDTH
    o_z = o_dn + 3 * DN_WIDTH
    o_ba = o_z + DN_WIDTH
    o_g = o_ba + 4 * DN_HEADS
    nh = DN_HEADS
    zeros_row = jnp.zeros((LANES,), _F32)

    for l in range(depth):
        def vec_for(sub, extra=None):
            rows = [norm_pre[l, sub], mod[l, sub, 0], mod[l, sub, 1], mod[l, sub, 2], norm_post[l, sub]]
            if extra is not None:
                rows.append(extra)
            return _vec(rows, d)

        x2 = _ffn_sublayer(x2, vec_for(0), ffn_w_in[l, 0].astype(_BF), ffn_w_out[l, 0].astype(_BF))

        wl = w_in[l]
        w_ba = jnp.pad(wl[:, o_ba:o_g], ((0, 0), (0, LANES - 4 * nh))).astype(_BF)
        wts = (wl[:, :o_dn].astype(_BF), wl[:, o_dn:o_z].astype(_BF), wl[:, o_z:o_ba].astype(_BF),
               wl[:, o_g:].astype(_BF), w_ba)
        neg_a = zeros_row.at[2 * nh:4 * nh].set(-jnp.exp(a_log[l].astype(_F32)).reshape(-1))
        dtb = zeros_row.at[2 * nh:4 * nh].set(dt_bias[l].astype(_F32).reshape(-1))
        dec = _vec([neg_a, dtb], LANES)
        qa, ka, va, (qd, kd, vd, z, gates, tab, tabt) = _mixer_projection(
            x2, vec_for(1), cos_t, sin_t, wts, conv_w[l], dec)
        att = [_dilated_attention(qa[g], ka[g], va[g], dil) for g, (_, dil) in enumerate(ATT_GROUPS)]
        o_f, o_b = _gated_delta(qd, kd, vd, tab, tabt)
        x2 = _mixer_output(x2, att, o_f, o_b, z, gates, vec_for(1, jnp.tile(dn_norm_w[l], d // DN_HEAD_DIM)),
                           w_branch_att[l].astype(_BF), w_branch_dn[l].astype(_BF), w_out[l].astype(_BF))

        x2 = _ffn_sublayer(x2, vec_for(2), ffn_w_in[l, 1].astype(_BF), ffn_w_out[l, 1].astype(_BF))
    return x2.reshape(b, s, d)
```

```python
import functools
import itertools

import jax
import jax.numpy as jnp
from jax import lax
from jax.experimental import pallas as pl
from jax.experimental.pallas import tpu as pltpu

D_MODEL = 1024
DEPTH = 2
N_SUBLAYERS = 3
ATT_GROUPS = ((128, 1), (512, 4), (2048, 16))
ATT_HEADS_PER_GROUP = 4
ATT_HEAD_DIM = 64
ATT_WIDTH = len(ATT_GROUPS) * ATT_HEADS_PER_GROUP * ATT_HEAD_DIM
ATT_GROUP_WIDTH = ATT_HEADS_PER_GROUP * ATT_HEAD_DIM
ATT_RADIUS = 64
ROPE_THETA = 10000.0
DN_HEADS = 6
DN_HEAD_DIM = 128
DN_WIDTH = DN_HEADS * DN_HEAD_DIM
DN_CONV = 5
FFN_DIM = 2816
EPS = 1e-6
NEG_INF = -1e30

LANES = 128
SUBLANES = 8
BF16_ROWS = 16
MXU_DIM = 256
VMEM_LIMIT_BYTES = 56 * 1024 * 1024

ROPE_PACK = LANES // (ATT_HEAD_DIM // 2)
ADA_COL_TILE = 2304
ROW_TILE = 512
FFN_CHUNKS = 2
FFN_ROW_TILE = 1024
FFN_ROW_SPLIT = 2
OUT_ROW_SPLIT = 2
PROJ_CHUNK = 256
ATT_Q_TILE = 1024
ATT_Q_SUB = 128
DN_CHUNK = 128
DN_BASE = 16
DN_GROUP = 6
HALO = BF16_ROWS

_BF = jnp.bfloat16
_F32 = jnp.float32


def _cparams(*sem):
    return pltpu.CompilerParams(dimension_semantics=sem, vmem_limit_bytes=VMEM_LIMIT_BYTES)


def _resident(shape):
    nd = len(shape)
    return pl.BlockSpec(shape, lambda *_: (0,) * nd, pipeline_mode=pl.Buffered(1))


def _rms(t):
    return t * lax.rsqrt(jnp.mean(t * t, axis=-1, keepdims=True) + EPS)


def _dot(a, b):
    return jnp.dot(a, b, preferred_element_type=_F32)


def _dot_nt(a, b):
    return lax.dot_general(a, b, (((1,), (1,)), ((), ())), preferred_element_type=_F32)


def _dot_tn(a, b):
    return lax.dot_general(a, b, (((0,), (0,)), ((), ())), preferred_element_type=_F32)


def _ada_kernel(c_ref, w_ref, b_ref, o_ref):
    c = c_ref[...]
    s = c * jax.nn.sigmoid(c)
    o_ref[0] = jnp.sum(w_ref[0] * s, axis=0, keepdims=True) + b_ref[0]


def _ada_modulation(c, w_ada, b_ada):
    depth, d, n = w_ada.shape
    tn = ADA_COL_TILE
    out = pl.pallas_call(
        _ada_kernel,
        grid=(depth, n // tn),
        in_specs=[
            pl.BlockSpec((d, 1), lambda l, j: (0, 0)),
            pl.BlockSpec((1, d, tn), lambda l, j: (l, 0, j)),
            pl.BlockSpec((1, 1, tn), lambda l, j: (l, 0, j)),
        ],
        out_specs=pl.BlockSpec((1, 1, tn), lambda l, j: (l, 0, j)),
        out_shape=jax.ShapeDtypeStruct((depth, 1, n), _F32),
        compiler_params=_cparams("arbitrary", "arbitrary"),
    )(c.reshape(d, 1), w_ada, b_ada.reshape(depth, 1, n))
    return out.reshape(depth, N_SUBLAYERS, 3, d)


def _rope_kernel(pos_ref, inv_ref, cos_ref, sin_ref):
    half = ATT_HEAD_DIM // 2
    pos = pos_ref[...].astype(_F32)
    rows = pos.shape[0]
    lane = lax.broadcasted_iota(jnp.int32, (rows, LANES), 1)
    grp = lane // half
    packed = pos[:, ROPE_PACK - 1:ROPE_PACK]
    for j in range(ROPE_PACK - 2, -1, -1):
        packed = jnp.where(grp == j, pos[:, j:j + 1], packed)
    ang = packed * inv_ref[...]
    first_half = (lane % ATT_HEAD_DIM) < half
    for val, out_ref, signed in ((jnp.cos(ang), cos_ref, False), (jnp.sin(ang), sin_ref, True)):
        for j in range(ROPE_PACK):
            y = jnp.where(grp == j, val, 0.0)
            shift = half
            while shift < LANES:
                y = y + pltpu.roll(y, shift, 1)
                shift *= 2
            if signed:
                y = jnp.where(first_half, -y, y)
            out_ref[pl.ds(j, rows, stride=ROPE_PACK), :] = y


def _rope_tables(positions):
    s = positions.shape[1]
    ts = min(2048, s)
    half = ATT_HEAD_DIM // 2
    inv = ROPE_THETA ** (-jnp.arange(half, dtype=_F32) * 2.0 / ATT_HEAD_DIM)
    inv_row = jnp.tile(inv, LANES // half).reshape(1, LANES)
    return pl.pallas_call(
        _rope_kernel,
        grid=(s // ts,),
        in_specs=[pl.BlockSpec((ts // ROPE_PACK, ROPE_PACK), lambda i: (i, 0)),
                  pl.BlockSpec((1, LANES), lambda i: (0, 0))],
        out_specs=[pl.BlockSpec((ts, LANES), lambda i: (i, 0))] * 2,
        out_shape=[jax.ShapeDtypeStruct((s, LANES), _F32)] * 2,
        compiler_params=_cparams("arbitrary"),
    )(positions.reshape(s // ROPE_PACK, ROPE_PACK), inv_row)


def _ada_pre(x, vec_ref):
    a = vec_ref[0:1, :] * (1.0 + vec_ref[2:3, :])
    return _rms(x) * a + vec_ref[1:2, :]


def _ffn_kernel(x_ref, vec_ref, win_ref, wout_ref, o_ref, h_ref, act_ref, *, tm):
    rows = [pl.ds(r * (tm // FFN_ROW_SPLIT), tm // FFN_ROW_SPLIT) for r in range(FFN_ROW_SPLIT)]
    tiles = FFN_DIM // MXU_DIM
    bounds = [MXU_DIM * (j * tiles // FFN_CHUNKS) for j in range(FFN_CHUNKS)] + [FFN_DIM]
    for r in rows:
        h_ref[r, :] = _ada_pre(x_ref[r, :], vec_ref).astype(_BF)
    for lo, hi in zip(bounds[:-1], bounds[1:]):
        for r in rows:
            g = _dot(h_ref[r, :], win_ref[:, lo:hi])
            u = _dot(h_ref[r, :], win_ref[:, FFN_DIM + lo:FFN_DIM + hi])
            act_ref[r, lo:hi] = (g * jax.nn.sigmoid(g) * u).astype(_BF)
    for r in rows:
        y = _dot(act_ref[r, :], wout_ref[...])
        o_ref[r, :] = x_ref[r, :] + (0.5 * vec_ref[3:4, :]) * (_rms(y) * vec_ref[4:5, :])


def _ffn_sublayer(x, vec, w_in, w_out):
    s, d = x.shape
    tm = min(FFN_ROW_TILE, s)
    return pl.pallas_call(
        functools.partial(_ffn_kernel, tm=tm),
        grid=(s // tm,),
        in_specs=[
            pl.BlockSpec((tm, d), lambda i: (i, 0)),
            _resident(vec.shape),
            _resident(w_in.shape),
            _resident(w_out.shape),
        ],
        out_specs=pl.BlockSpec((tm, d), lambda i: (i, 0)),
        out_shape=jax.ShapeDtypeStruct((s, d), _F32),
        scratch_shapes=[pltpu.VMEM((tm, d), _BF), pltpu.VMEM((tm, FFN_DIM), _BF)],
        compiler_params=_cparams("arbitrary"),
    )(x, vec, w_in, w_out)


def _swap_halves(t):
    n = t.shape[1]
    half = ATT_HEAD_DIM // 2
    from_right = pltpu.roll(t, n - half, 1)
    from_left = pltpu.roll(t, half, 1)
    lane = lax.broadcasted_iota(jnp.int32, t.shape, 1)
    return jnp.where((lane % ATT_HEAD_DIM) < half, from_right, from_left)


def _chunk_cumsums(g, tm):
    row = lax.broadcasted_iota(jnp.int32, g.shape, 0) % DN_CHUNK
    fwd, bwd = g, g
    sh = 1
    while sh < DN_CHUNK:
        fwd = fwd + jnp.where(row >= sh, pltpu.roll(fwd, sh, 0), 0.0)
        bwd = bwd + jnp.where(row < DN_CHUNK - sh, pltpu.roll(bwd, tm - sh, 0), 0.0)
        sh *= 2
    return fwd, bwd


def _store_by_class(val, out_ref, stage_ref, dil, tm):
    gw = ATT_GROUP_WIDTH
    if dil == 1:
        out_ref[...] = val.astype(_BF)
        return
    for c in range(gw // LANES):
        stage_ref[c] = val[:, c * LANES:(c + 1) * LANES]
    for r in range(dil):
        for c in range(gw // LANES):
            out_ref[:, r * gw + c * LANES:r * gw + (c + 1) * LANES] = (
                stage_ref[c, pl.ds(r, tm // dil, stride=dil), :].astype(_BF))


def _proj_kernel(xp_ref, x_ref, xn_ref, vec_ref, cos_ref, sin_ref, watt_ref, wdn_ref, wz_ref, wg_ref, wba_ref,
                 conv_ref, dec_ref,
                 q0_ref, q1_ref, q2_ref, k0_ref, k1_ref, k2_ref, v0_ref, v1_ref, v2_ref,
                 qd_ref, kd_ref, vd_ref, z_ref, gate_ref, tab_ref, tabt_ref,
                 h_ref, pdn_ref, dnst_ref, stage_ref, *, tm):
    i = pl.program_id(0)
    last = pl.num_programs(0) - 1
    cw = PROJ_CHUNK
    gw = ATT_GROUP_WIDTH
    x_ext = jnp.concatenate([xp_ref[0], x_ref[...], xn_ref[0]], axis=0)
    h_ref[...] = _ada_pre(x_ext, vec_ref).astype(_BF)
    inner = pl.ds(HALO, tm)
    cos = jnp.tile(cos_ref[...], (1, cw // LANES))
    sin = jnp.tile(sin_ref[...], (1, cw // LANES))
    row = lax.broadcasted_iota(jnp.int32, (tm + 2 * HALO, 1), 0)
    outside = ((i == 0) & (row < HALO)) | ((i == last) & (row >= HALO + tm))
    pad = DN_CONV // 2
    att_out = ((q0_ref, q1_ref, q2_ref), (k0_ref, k1_ref, k2_ref), (v0_ref, v1_ref, v2_ref))
    dn_out = (qd_ref, kd_ref, vd_ref)

    def att_task(kind, g):
        col = kind * ATT_WIDTH + g * gw

        def epilogue(t):
            if kind < 2:
                t = t * cos + _swap_halves(t) * sin
            if kind == 0:
                t = t * (ATT_HEAD_DIM ** -0.5)
            _store_by_class(t, att_out[kind][g], stage_ref.at[kind], ATT_GROUPS[g][1], tm)
        return (lambda: _dot(h_ref[inner, :], watt_ref[:, col:col + cw])), epilogue

    def dn_task(c):
        kind, col = divmod(c * cw, DN_WIDTH)
        slot = c % 2

        def epilogue(t):
            p = jnp.where(outside, 0.0, t)
            for hd in range(cw // DN_HEAD_DIM):
                hs = slice(hd * DN_HEAD_DIM, (hd + 1) * DN_HEAD_DIM)
                pdn_ref[slot, hd] = p[:, hs]
                wc = conv_ref[:, c * cw + hd * DN_HEAD_DIM:c * cw + (hd + 1) * DN_HEAD_DIM]
                for phase in range(2):
                    acc = None
                    for tap in range(DN_CONV):
                        rows = pdn_ref[slot, hd, pl.ds(HALO - pad + tap + phase, tm // 2, stride=2), :]
                        acc = rows * wc[tap:tap + 1, :] if acc is None else acc + rows * wc[tap:tap + 1, :]
                    t_h = acc * jax.nn.sigmoid(acc)
                    if kind < 2:
                        t_h = t_h * lax.rsqrt(jnp.sum(t_h * t_h, axis=-1, keepdims=True) + EPS)
                    if kind == 0:
                        t_h = t_h * (DN_HEAD_DIM ** -0.5)
                    dnst_ref[slot, hd, pl.ds(phase, tm // 2, stride=2), :] = t_h
                dn_out[kind][:, col + hd * DN_HEAD_DIM:col + (hd + 1) * DN_HEAD_DIM] = dnst_ref[slot, hd].astype(_BF)
        return (lambda: _dot(h_ref[...], wdn_ref[:, c * cw:(c + 1) * cw])), epilogue

    def z_task(c):
        def epilogue(t):
            z_ref[:, c * cw:(c + 1) * cw] = (t * jax.nn.sigmoid(t)).astype(_BF)
        return (lambda: _dot(h_ref[inner, :], wz_ref[:, c * cw:(c + 1) * cw])), epilogue

    def gate_task(c):
        def epilogue(t):
            gate_ref[:, c * cw:(c + 1) * cw] = jax.nn.sigmoid(t).astype(_BF)
        return (lambda: _dot(h_ref[inner, :], wg_ref[:, c * cw:(c + 1) * cw])), epilogue

    def table_task():
        def epilogue(raw):
            lane = lax.broadcasted_iota(jnp.int32, raw.shape, 1)
            beta = jax.nn.sigmoid(raw)
            log_decay = dec_ref[0:1, :] * jax.nn.softplus(raw + dec_ref[1:2, :])
            cum_f, cum_b = _chunk_cumsums(log_decay, tm)
            nh = DN_HEADS
            tab = jnp.where(lane < 2 * nh, beta, jnp.where(lane < 3 * nh, cum_f, jnp.where(lane < 4 * nh, cum_b, 0.0)))
            tab_ref[...] = tab
            tabt_ref[...] = tab.T[:4 * nh, :]
        return (lambda: _dot(h_ref[inner, :], wba_ref[...])), epilogue

    light = ([att_task(kind, g) for kind in range(3) for g in range(len(ATT_GROUPS))]
             + [z_task(c) for c in range(DN_WIDTH // cw)] + [gate_task(c) for c in range(2 * D_MODEL // cw)]
             + [table_task()])
    heavy = [dn_task(c) for c in range(3 * DN_WIDTH // cw)]
    tasks = []
    for n, task in enumerate(heavy):
        tasks += [task] + light[n * len(light) // len(heavy):(n + 1) * len(light) // len(heavy)]
    pending = None
    for matmul, epilogue in tasks:
        val = matmul()
        if pending is not None:
            pending[1](pending[0])
        pending = (val, epilogue)
    pending[1](pending[0])


def _mixer_projection(x, vec, cos_t, sin_t, wts, conv_w, dec):
    s, d = x.shape
    tm = min(ROW_TILE, s)
    nt = s // tm
    hb = tm // HALO
    x3 = x.reshape(s // HALO, HALO, d)
    row = lambda i: (i, 0)
    w_att, w_dn, w_z, w_g, w_ba = wts
    bf = lambda n: jax.ShapeDtypeStruct((s, n), _BF)
    gw = ATT_GROUP_WIDTH
    att_specs = [pl.BlockSpec((tm // dil, dil * gw), row) for _, dil in ATT_GROUPS] * 3
    att_shapes = [jax.ShapeDtypeStruct((s // dil, dil * gw), _BF) for _, dil in ATT_GROUPS] * 3
    outs = pl.pallas_call(
        functools.partial(_proj_kernel, tm=tm),
        grid=(nt,),
        in_specs=[
            pl.BlockSpec((1, HALO, d), lambda i: (jnp.maximum(i * hb - 1, 0), 0, 0)),
            pl.BlockSpec((tm, d), row),
            pl.BlockSpec((1, HALO, d), lambda i: (jnp.minimum((i + 1) * hb, s // HALO - 1), 0, 0)),
            _resident(vec.shape),
            pl.BlockSpec((tm, LANES), row),
            pl.BlockSpec((tm, LANES), row),
            _resident(w_att.shape), _resident(w_dn.shape), _resident(w_z.shape), _resident(w_g.shape),
            _resident(w_ba.shape), _resident(conv_w.shape), _resident(dec.shape),
        ],
        out_specs=att_specs + [
            pl.BlockSpec((tm, DN_WIDTH), row), pl.BlockSpec((tm, DN_WIDTH), row), pl.BlockSpec((tm, DN_WIDTH), row),
            pl.BlockSpec((tm, DN_WIDTH), row), pl.BlockSpec((tm, 2 * d), row),
            pl.BlockSpec((tm, LANES), row), pl.BlockSpec((4 * DN_HEADS, tm), lambda i: (0, i)),
        ],
        out_shape=att_shapes + [bf(DN_WIDTH), bf(DN_WIDTH), bf(DN_WIDTH), bf(DN_WIDTH), bf(2 * d),
                                jax.ShapeDtypeStruct((s, LANES), _F32), jax.ShapeDtypeStruct((4 * DN_HEADS, s), _F32)],
        scratch_shapes=[pltpu.VMEM((tm + 2 * HALO, d), _BF),
                        pltpu.VMEM((2, PROJ_CHUNK // DN_HEAD_DIM, tm + 2 * HALO, DN_HEAD_DIM), _F32),
                        pltpu.VMEM((2, PROJ_CHUNK // DN_HEAD_DIM, tm, DN_HEAD_DIM), _F32),
                        pltpu.VMEM((3, gw // LANES, tm, LANES), _F32)],
        compiler_params=_cparams("arbitrary"),
    )(x3, x, x3, vec, cos_t, sin_t, w_att, w_dn, w_z, w_g, w_ba, conv_w, dec)
    return outs[0:3], outs[3:6], outs[6:9], outs[9:]


def _att_kernel(q_ref, kp_ref, k_ref, kn_ref, vp_ref, v_ref, vn_ref, o_ref, lse_ref, *, nq, n_rows):
    i = pl.program_id(1)
    r = ATT_RADIUS
    qs = ATT_Q_SUB
    kw = qs + 2 * r
    nsub = nq // qs
    qi = lax.broadcasted_iota(jnp.int32, (2 * qs, kw), 0) % qs
    kj = lax.broadcasted_iota(jnp.int32, (2 * qs, kw), 1) - r
    band = jnp.where(jnp.abs(kj - qi) <= r, 0.0, NEG_INF)
    lane = lax.broadcasted_iota(jnp.int32, (1, LANES), 1)
    low = lane < ATT_HEAD_DIM
    kcol = lax.broadcasted_iota(jnp.int32, (1, kw), 1)

    def window(prev_ref, main_ref, next_ref, j, cs):
        lo, hi = j * qs - r, j * qs + qs + r
        parts = [prev_ref[:, cs]] if lo < 0 else []
        parts.append(main_ref[max(lo, 0):min(hi, nq), cs])
        if hi > nq:
            parts.append(next_ref[:, cs])
        return parts[0] if len(parts) == 1 else jnp.concatenate(parts, axis=0)

    for j in range(nsub):
        r0 = j * qs
        bias = band
        if j == 0 or j == nsub - 1:
            kidx = i * nq + r0 - r + kcol
            bias = band + jnp.where((kidx < 0) | (kidx >= n_rows), NEG_INF, 0.0)
        for hp in range(ATT_GROUP_WIDTH // LANES):
            cs = slice(hp * LANES, (hp + 1) * LANES)
            qp = q_ref[r0:r0 + qs, cs]
            kp = window(kp_ref, k_ref, kn_ref, j, cs)
            vp = window(vp_ref, v_ref, vn_ref, j, cs)
            zero = jnp.zeros_like(qp)
            qq = jnp.concatenate([jnp.where(low, qp, zero), jnp.where(low, zero, qp)], axis=0)
            sc = _dot_nt(qq, kp) + bias
            m = jnp.max(sc, axis=-1, keepdims=True)
            p = jnp.exp(sc - m)
            den = jnp.sum(p, axis=-1, keepdims=True)
            pv = _dot(p.astype(_BF), vp) / den
            lse = m + jnp.log(den)
            o_ref[r0:r0 + qs, cs] = jnp.where(low, pv[:qs], pv[qs:]).astype(_BF)
            lse_ref[r0:r0 + qs, cs] = jnp.where(low, lse[:qs], lse[qs:])


def _dilated_attention(qv, kv, vv, dilation):
    n_rows = qv.shape[0]
    nq = min(ATT_Q_TILE, n_rows)
    nt = n_rows // nq
    r = ATT_RADIUS
    gw = ATT_GROUP_WIDTH
    main = pl.BlockSpec((nq, gw), lambda c, i: (i, c))
    prev = pl.BlockSpec((r, gw), lambda c, i: (jnp.maximum(i * (nq // r) - 1, 0), c))
    nxt = pl.BlockSpec((r, gw), lambda c, i: (jnp.minimum((i + 1) * (nq // r), n_rows // r - 1), c))
    out = main
    return pl.pallas_call(
        functools.partial(_att_kernel, nq=nq, n_rows=n_rows),
        grid=(dilation, nt),
        in_specs=[main, prev, main, nxt, prev, main, nxt],
        out_specs=[out, out],
        out_shape=[jax.ShapeDtypeStruct((n_rows, dilation * gw), _BF),
                   jax.ShapeDtypeStruct((n_rows, dilation * gw), _F32)],
        compiler_params=_cparams("arbitrary", "arbitrary"),
    )(qv, kv, kv, kv, vv, vv, vv)


def _diag_blocks_by_lane(m):
    c = m.shape[0]
    b = DN_BASE
    blk = lax.broadcasted_iota(jnp.int32, (b, c), 1) // b
    dg = jnp.zeros((b, c), _F32)
    for bi in range(c // b):
        dg = jnp.where(blk == bi, m[bi * b:(bi + 1) * b, :], dg)
    return dg


def _blocks_to_diagonal(acc):
    b, c = acc.shape
    blk = lax.broadcasted_iota(jnp.int32, (b, c), 1) // b
    return jnp.concatenate([jnp.where(blk == bi, acc, 0.0) for bi in range(c // b)], axis=0)


def _pair_block_diag(t):
    c = t.shape[0]
    left = lax.broadcasted_iota(jnp.int32, (1, 2 * c), 1) < c
    zero = jnp.zeros_like(t)
    return jnp.concatenate([jnp.where(left, t, zero), jnp.where(left, zero, t)], axis=0)


def _dn_kernel(qf_ref, kf_ref, vf_ref, tf_ref, ttf_ref, qb_ref, kb_ref, vb_ref, tb_ref, ttb_ref,
               of_ref, ob_ref, state_ref, m_ref, attn_ref, x_ref, vbeta_ref, kbeta_ref, qg_ref, kdec_ref, gl_ref):
    step = pl.program_id(0)
    slots = (m_ref, attn_ref, x_ref, vbeta_ref, kbeta_ref, qg_ref, kdec_ref, gl_ref)

    @pl.when(step == 0)
    def _():
        state_ref[...] = jnp.zeros_like(state_ref)
        for ref in slots:
            ref[...] = jnp.zeros_like(ref)

    wr = step % 2
    rd = 1 - wr
    c = DN_CHUNK
    nh = DN_HEADS
    hd_w = DN_HEAD_DIM
    b = DN_BASE
    npairs = nh
    ri = lax.broadcasted_iota(jnp.int32, (c, 2 * c), 0)
    lane2 = lax.broadcasted_iota(jnp.int32, (c, 2 * c), 1)
    ci = lane2 % c
    left = lane2 < c

    def pair_cols(tab, lane_a):
        return jnp.where(left, tab[:, lane_a:lane_a + 1], tab[:, lane_a + 1:lane_a + 2])

    def pair_slice(p):
        hd = (p % (nh // 2)) * 2
        return slice(hd * hd_w, (hd + 2) * hd_w)

    def table_lanes(p):
        lb = (p // (nh // 2)) * nh + (p % (nh // 2)) * 2
        return lb, 2 * nh + lb

    def prepare():
        ms = []
        for direction, (q_ref, k_ref, v_ref, t_ref, tt_ref) in enumerate(
                ((qf_ref, kf_ref, vf_ref, tf_ref, ttf_ref), (qb_ref, kb_ref, vb_ref, tb_ref, ttb_ref))):
            upper = direction == 1
            incl = (ri <= ci) if upper else (ri >= ci)
            strict = (ri < ci) if upper else (ri > ci)
            tab = t_ref[...]
            tabt = tt_ref[...]
            for p in range(direction * nh // 2, (direction + 1) * nh // 2):
                cs = pair_slice(p)
                lb, lg = table_lanes(p)
                kp = k_ref[:, cs]
                beta = pair_cols(tab, lb)
                gcol = pair_cols(tab, lg)
                grow = jnp.concatenate([tabt[lg:lg + 1, :], tabt[lg + 1:lg + 2, :]], axis=1)
                decay = jnp.exp(jnp.where(incl, gcol - grow, NEG_INF))
                prod = _dot_nt(jnp.concatenate([q_ref[:, cs], kp], axis=0), _pair_block_diag(kp))
                m = jnp.where(strict, prod[c:] * beta * decay, 0.0)
                m_ref[wr, p] = m.astype(_BF)
                attn_ref[wr, p] = (prod[:c] * decay).astype(_BF)
                ms.append(m)
                yield
                k = kp.astype(_F32)
                glast = gcol[0:1, :] if upper else gcol[c - 1:c, :]
                eg = jnp.exp(gcol)
                vbeta_ref[wr, p] = (v_ref[:, cs].astype(_F32) * beta).astype(_BF)
                kbeta_ref[wr, p] = (k * (beta * eg)).astype(_BF)
                qg_ref[wr, p] = (q_ref[:, cs].astype(_F32) * eg).astype(_BF)
                kdec_ref[wr, p] = (k * jnp.exp(glast - gcol)).astype(_BF)
                gl_ref[wr, p] = jnp.broadcast_to(jnp.exp(glast), (SUBLANES, 2 * c))
                yield
        n = 2 * npairs
        dg_all = jnp.concatenate([_diag_blocks_by_lane(m[:, h * c:(h + 1) * c]) for m in ms for h in range(2)], axis=0)
        half = n * b // 2
        dg_lo, dg_up = dg_all[:half], dg_all[half:]
        blk_half = lax.broadcasted_iota(jnp.int32, (half, c), 1) // b

        def column(dg, j):
            return jnp.take_along_axis(dg, blk_half * b + j, axis=1).reshape(n // 2, b, c)

        lane = lax.broadcasted_iota(jnp.int32, (b, c), 1)
        sub = lax.broadcasted_iota(jnp.int32, (b, c), 0)
        eye = jnp.broadcast_to(jnp.where(sub == lane % b, 1.0, 0.0), (n // 2, b, c))
        lo, up = eye, eye
        for t in range(b - 1):
            lo = lo - column(dg_lo, t) * lo[:, t:t + 1, :]
            up = up - column(dg_up, b - 1 - t) * up[:, b - 1 - t:b - t, :]
            if t % 3 == 2:
                yield
        for p in range(npairs):
            acc = lo if p < npairs // 2 else up
            r0 = 2 * (p % (npairs // 2))
            x_ref[wr, p] = jnp.concatenate(
                [_blocks_to_diagonal(acc[r0]), _blocks_to_diagonal(acc[r0 + 1])], axis=1).astype(_BF)
        yield

    def recur(group):
        ms = {p: m_ref[rd, p] for p in group}
        xs = {p: x_ref[rd, p] for p in group}
        size = b
        one = jnp.ones((), _BF)
        while size < c:
            off_diag = (ri // (2 * size) == ci // (2 * size)) & (ri // size != ci // size)
            zero = jnp.zeros_like(ms[group[0]])
            ts = {p: _dot(xs[p], _pair_block_diag(jnp.where(off_diag, ms[p], zero))).astype(_BF) for p in group}
            yield
            xs = {p: _dot(jnp.where(ri == ci, one, -ts[p]), _pair_block_diag(xs[p])).astype(_BF) for p in group}
            yield
            size *= 2
        us, wqs = {}, {}
        for p in group:
            vbeta, kbeta = vbeta_ref[rd, p], kbeta_ref[rd, p]
            sols = [_dot(xs[p][:, h * c:(h + 1) * c],
                         jnp.concatenate([vbeta[:, h * hd_w:(h + 1) * hd_w], kbeta[:, h * hd_w:(h + 1) * hd_w]], axis=1))
                    for h in range(2)]
            us[p] = jnp.concatenate([sols[0][:, :hd_w], sols[1][:, :hd_w]], axis=1)
            w = jnp.concatenate([sols[0][:, hd_w:], sols[1][:, hd_w:]], axis=1)
            wqs[p] = jnp.concatenate([w.astype(_BF), qg_ref[rd, p]], axis=0)
            yield
        states = {p: state_ref[p] for p in group}
        wss = {p: _dot(wqs[p], _pair_block_diag(states[p].astype(_BF))) for p in group}
        yield
        vbs = {p: (us[p] - wss[p][:c]).astype(_BF) for p in group}
        for p in group:
            o_ref = of_ref if p < npairs // 2 else ob_ref
            o_ref[:, pair_slice(p)] = wss[p][c:] + _dot(attn_ref[rd, p], _pair_block_diag(vbs[p]))
        yield
        for p in group:
            kdec = kdec_ref[rd, p]
            upd = [_dot_tn(kdec[:, h * hd_w:(h + 1) * hd_w], vbs[p][:, h * hd_w:(h + 1) * hd_w]) for h in range(2)]
            state_ref[p] = states[p] * gl_ref[rd, p, 0:1, :] + jnp.concatenate(upd, axis=1)
        yield

    groups = [list(range(g, g + DN_GROUP)) for g in range(0, npairs, DN_GROUP)]
    for _ in itertools.zip_longest(itertools.chain(*(recur(g) for g in groups)), prepare()):
        pass


def _gated_delta(qd, kd, vd, tab, tabt):
    s = qd.shape[0]
    c = DN_CHUNK
    n = s // c
    npairs = DN_HEADS
    def chunk_of(stream, flip):
        def index(i):
            j = jnp.maximum(i - 1, 0) if stream == "rec" else jnp.minimum(i, n - 1)
            return n - 1 - j if flip else j
        return index

    def wide(stream, flip):
        index = chunk_of(stream, flip)
        return pl.BlockSpec((c, DN_WIDTH), lambda i: (index(i), 0))

    def table(flip):
        index = chunk_of("prep", flip)
        return pl.BlockSpec((c, LANES), lambda i: (index(i), 0))

    def table_t(flip):
        index = chunk_of("prep", flip)
        return pl.BlockSpec((4 * DN_HEADS, c), lambda i: (0, index(i)))

    specs = []
    for flip in (False, True):
        specs += [wide("prep", flip), wide("prep", flip), wide("prep", flip), table(flip), table_t(flip)]
    slot = pltpu.VMEM((2, npairs, c, 2 * c), _BF)
    return pl.pallas_call(
        _dn_kernel,
        grid=(n + 1,),
        in_specs=specs,
        out_specs=[wide("rec", False), wide("rec", True)],
        out_shape=[jax.ShapeDtypeStruct((s, DN_WIDTH), _F32)] * 2,
        scratch_shapes=[pltpu.VMEM((npairs, DN_HEAD_DIM, 2 * DN_HEAD_DIM), _F32),
                        slot, slot, slot,
                        slot, slot, slot, slot,
                        pltpu.VMEM((2, npairs, SUBLANES, 2 * c), _F32)],
        compiler_params=_cparams("arbitrary"),
    )(qd, kd, vd, tab, tabt, qd, kd, vd, tab, tabt)


def _token_order(blk_ref, stage_ref, dil, row0, rows):
    gw = ATT_GROUP_WIDTH
    src = pl.ds(row0 // dil, rows // dil)
    if dil == 1:
        return blk_ref[src, :].astype(_F32)
    for r in range(dil):
        for c in range(gw // LANES):
            stage_ref[c, pl.ds(row0 + r, rows // dil, stride=dil), :] = (
                blk_ref[src, r * gw + c * LANES:r * gw + (c + 1) * LANES].astype(_F32))
    return jnp.concatenate([stage_ref[c, pl.ds(row0, rows), :] for c in range(gw // LANES)], axis=1)


def _out_kernel(x_ref, o0_ref, o1_ref, o2_ref, l0_ref, l1_ref, l2_ref, of_ref, ob_ref, z_ref, gate_ref, vec_ref,
                wa_ref, wd_ref, wo_ref, out_ref, oatt_ref, odn_ref, y_ref, stage_ref, *, tm):
    d = D_MODEL
    dils = [dil for _, dil in ATT_GROUPS]
    rows = tm // OUT_ROW_SPLIT
    subs = [(r * rows, pl.ds(r * rows, rows)) for r in range(OUT_ROW_SPLIT)]
    for row0, rs in subs:
        o0, o1, o2 = (_token_order(ref, stage_ref.at[n], dils[n], row0, rows)
                      for n, ref in enumerate((o0_ref, o1_ref, o2_ref)))
        l0, l1, l2 = (_token_order(ref, stage_ref.at[3 + n], dils[n], row0, rows)
                      for n, ref in enumerate((l0_ref, l1_ref, l2_ref)))
        mx = jnp.maximum(jnp.maximum(l0, l1), l2)
        e0, e1, e2 = jnp.exp(l0 - mx), jnp.exp(l1 - mx), jnp.exp(l2 - mx)
        oatt_ref[rs, :] = ((e0 * o0 + e1 * o1 + e2 * o2) / (e0 + e1 + e2)).astype(_BF)
        for hd in range(DN_HEADS):
            sl = slice(hd * DN_HEAD_DIM, (hd + 1) * DN_HEAD_DIM)
            o = of_ref[rs, sl] + ob_ref[rs, sl]
            o = _rms(o) * vec_ref[5:6, 0:DN_HEAD_DIM]
            odn_ref[rs, sl] = (o * z_ref[rs, sl].astype(_F32)).astype(_BF)
    for row0, rs in subs:
        ya = _dot(oatt_ref[rs, :], wa_ref[...])
        yd = _dot(odn_ref[rs, :], wd_ref[...])
        y_ref[rs, :] = (gate_ref[rs, :d].astype(_F32) * ya + gate_ref[rs, d:].astype(_F32) * yd).astype(_BF)
    for row0, rs in subs:
        y = _dot(y_ref[rs, :], wo_ref[...])
        out_ref[rs, :] = x_ref[rs, :] + vec_ref[3:4, :] * (_rms(y) * vec_ref[4:5, :])


def _mixer_output(x, att, o_f, o_b, z, gates, vec, w_a, w_d, w_o):
    s, d = x.shape
    tm = min(ROW_TILE, s)
    row = lambda i: (i, 0)
    (o0, l0), (o1, l1), (o2, l2) = att
    gw = ATT_GROUP_WIDTH
    by_class = [pl.BlockSpec((tm // dil, dil * gw), row) for _, dil in ATT_GROUPS]
    return pl.pallas_call(
        functools.partial(_out_kernel, tm=tm),
        grid=(s // tm,),
        in_specs=[pl.BlockSpec((tm, d), row)] + by_class * 2
        + [pl.BlockSpec((tm, DN_WIDTH), row)] * 3 + [pl.BlockSpec((tm, 2 * d), row)]
        + [_resident(vec.shape), _resident(w_a.shape), _resident(w_d.shape), _resident(w_o.shape)],
        out_specs=pl.BlockSpec((tm, d), row),
        out_shape=jax.ShapeDtypeStruct((s, d), _F32),
        scratch_shapes=[pltpu.VMEM((tm, gw), _BF), pltpu.VMEM((tm, DN_WIDTH), _BF), pltpu.VMEM((tm, d), _BF),
                        pltpu.VMEM((6, gw // LANES, tm, LANES), _F32)],
        compiler_params=_cparams("arbitrary"),
    )(x, o0, o1, o2, l0, l1, l2, o_f, o_b, z, gates, vec, w_a, w_d, w_o)


def _vec(rows, d):
    pad = [jnp.zeros((d,), _F32)] * (SUBLANES - len(rows))
    return jnp.stack(list(rows) + pad, axis=0)


def kernel(x, c, positions, w_ada, b_ada, norm_pre, norm_post, ffn_w_in, ffn_w_out, w_in, conv_w, a_log, dt_bias,
           dn_norm_w, w_branch_att, w_branch_dn, w_out):
    b, s, d = x.shape
    assert b == 1 and d == D_MODEL and s % (DN_CHUNK * 16) == 0
    depth = w_ada.shape[0]
    x2 = x.reshape(s, d)
    mod = _ada_modulation(c, w_ada, b_ada)
    cos_t, sin_t = _rope_tables(positions)

    o_dn = 3 * ATT_WIDTH
    o_z = o_dn + 3 * DN_WIDTH
    o_ba = o_z + DN_WIDTH
    o_g = o_ba + 4 * DN_HEADS
    nh = DN_HEADS
    zeros_row = jnp.zeros((LANES,), _F32)

    for l in range(depth):
        def vec_for(sub, extra=None):
            rows = [norm_pre[l, sub], mod[l, sub, 0], mod[l, sub, 1], mod[l, sub, 2], norm_post[l, sub]]
            if extra is not None:
                rows.append(extra)
            return _vec(rows, d)

        x2 = _ffn_sublayer(x2, vec_for(0), ffn_w_in[l, 0].astype(_BF), ffn_w_out[l, 0].astype(_BF))

        wl = w_in[l]
        w_ba = jnp.pad(wl[:, o_ba:o_g], ((0, 0), (0, LANES - 4 * nh))).astype(_BF)
        wts = (wl[:, :o_dn].astype(_BF), wl[:, o_dn:o_z].astype(_BF), wl[:, o_z:o_ba].astype(_BF),
               wl[:, o_g:].astype(_BF), w_ba)
        neg_a = zeros_row.at[2 * nh:4 * nh].set(-jnp.exp(a_log[l].astype(_F32)).reshape(-1))
        dtb = zeros_row.at[2 * nh:4 * nh].set(dt_bias[l].astype(_F32).reshape(-1))
        dec = _vec([neg_a, dtb], LANES)
        qa, ka, va, (qd, kd, vd, z, gates, tab, tabt) = _mixer_projection(
            x2, vec_for(1), cos_t, sin_t, wts, conv_w[l], dec)
        att = [_dilated_attention(qa[g], ka[g], va[g], dil) for g, (_, dil) in enumerate(ATT_GROUPS)]
        o_f, o_b = _gated_delta(qd, kd, vd, tab, tabt)
        x2 = _mixer_output(x2, att, o_f, o_b, z, gates, vec_for(1, jnp.tile(dn_norm_w[l], d // DN_HEAD_DIM)),
                           w_branch_att[l].astype(_BF), w_branch_dn[l].astype(_BF), w_out[l].astype(_BF))

        x2 = _ffn_sublayer(x2, vec_for(2), ffn_w_in[l, 1].astype(_BF), ffn_w_out[l, 1].astype(_BF))
    return x2.reshape(b, s, d)
```

```python
import functools
import itertools

import jax
import jax.numpy as jnp
from jax import lax
from jax.experimental import pallas as pl
from jax.experimental.pallas import tpu as pltpu

D_MODEL = 1024
DEPTH = 2
N_SUBLAYERS = 3
ATT_GROUPS = ((128, 1), (512, 4), (2048, 16))
ATT_HEADS_PER_GROUP = 4
ATT_HEAD_DIM = 64
ATT_WIDTH = len(ATT_GROUPS) * ATT_HEADS_PER_GROUP * ATT_HEAD_DIM
ATT_GROUP_WIDTH = ATT_HEADS_PER_GROUP * ATT_HEAD_DIM
ATT_RADIUS = 64
ROPE_THETA = 10000.0
DN_HEADS = 6
DN_HEAD_DIM = 128
DN_WIDTH = DN_HEADS * DN_HEAD_DIM
DN_CONV = 5
FFN_DIM = 2816
EPS = 1e-6
NEG_INF = -1e30

LANES = 128
SUBLANES = 8
BF16_ROWS = 16
MXU_DIM = 256
VMEM_LIMIT_BYTES = 56 * 1024 * 1024

ROPE_PACK = LANES // (ATT_HEAD_DIM // 2)
ADA_ROW_TILE = 256
ROW_TILE = 512
FFN_CHUNKS = 2
FFN_ROW_TILE = 1024
FFN_ROW_SPLIT = 2
OUT_ROW_SPLIT = 2
PROJ_CHUNK = 256
ATT_Q_TILE = 1024
ATT_Q_SUB = 128
DN_CHUNK = 128
DN_BASE = 16
DN_STEP_CHUNKS = 2
DN_GROUP = 6
HALO = BF16_ROWS

_BF = jnp.bfloat16
_F32 = jnp.float32


def _cparams(*sem):
    return pltpu.CompilerParams(dimension_semantics=sem, vmem_limit_bytes=VMEM_LIMIT_BYTES)


def _resident(shape):
    nd = len(shape)
    return pl.BlockSpec(shape, lambda *_: (0,) * nd, pipeline_mode=pl.Buffered(1))


def _rms(t):
    return t * lax.rsqrt(jnp.mean(t * t, axis=-1, keepdims=True) + EPS)


def _dot(a, b):
    return jnp.dot(a, b, preferred_element_type=_F32)


def _dot_nt(a, b):
    return lax.dot_general(a, b, (((1,), (1,)), ((), ())), preferred_element_type=_F32)


def _dot_tn(a, b):
    return lax.dot_general(a, b, (((0,), (0,)), ((), ())), preferred_element_type=_F32)


def _ada_kernel(c_ref, w_ref, b_ref, o_ref):
    c = c_ref[...]
    s = c * jax.nn.sigmoid(c)
    part = jnp.sum(w_ref[0] * s, axis=0, keepdims=True)

    @pl.when(pl.program_id(1) == 0)
    def _():
        o_ref[0] = b_ref[0] + part

    @pl.when(pl.program_id(1) != 0)
    def _():
        o_ref[0] = o_ref[0] + part


def _ada_modulation(c, w_ada, b_ada):
    depth, d, n = w_ada.shape
    tk = ADA_ROW_TILE
    out = pl.pallas_call(
        _ada_kernel,
        grid=(depth, d // tk),
        in_specs=[
            pl.BlockSpec((tk, 1), lambda l, k: (k, 0)),
            pl.BlockSpec((1, tk, n), lambda l, k: (l, k, 0)),
            pl.BlockSpec((1, 1, n), lambda l, k: (l, 0, 0)),
        ],
        out_specs=pl.BlockSpec((1, 1, n), lambda l, k: (l, 0, 0)),
        out_shape=jax.ShapeDtypeStruct((depth, 1, n), _F32),
        compiler_params=_cparams("arbitrary", "arbitrary"),
    )(c.reshape(d, 1), w_ada, b_ada.reshape(depth, 1, n))
    return out.reshape(depth, N_SUBLAYERS, 3, d)


def _rope_kernel(pos_ref, inv_ref, cos_ref, sin_ref):
    half = ATT_HEAD_DIM // 2
    pos = pos_ref[...].astype(_F32)
    rows = pos.shape[0]
    lane = lax.broadcasted_iota(jnp.int32, (rows, LANES), 1)
    grp = lane // half
    packed = pos[:, ROPE_PACK - 1:ROPE_PACK]
    for j in range(ROPE_PACK - 2, -1, -1):
        packed = jnp.where(grp == j, pos[:, j:j + 1], packed)
    ang = packed * inv_ref[...]
    first_half = (lane % ATT_HEAD_DIM) < half
    for val, out_ref, signed in ((jnp.cos(ang), cos_ref, False), (jnp.sin(ang), sin_ref, True)):
        for j in range(ROPE_PACK):
            y = jnp.where(grp == j, val, 0.0)
            shift = half
            while shift < LANES:
                y = y + pltpu.roll(y, shift, 1)
                shift *= 2
            if signed:
                y = jnp.where(first_half, -y, y)
            out_ref[pl.ds(j, rows, stride=ROPE_PACK), :] = y


def _rope_tables(positions):
    s = positions.shape[1]
    ts = min(2048, s)
    half = ATT_HEAD_DIM // 2
    inv = ROPE_THETA ** (-jnp.arange(half, dtype=_F32) * 2.0 / ATT_HEAD_DIM)
    inv_row = jnp.tile(inv, LANES // half).reshape(1, LANES)
    return pl.pallas_call(
        _rope_kernel,
        grid=(s // ts,),
        in_specs=[pl.BlockSpec((ts // ROPE_PACK, ROPE_PACK), lambda i: (i, 0)),
                  pl.BlockSpec((1, LANES), lambda i: (0, 0))],
        out_specs=[pl.BlockSpec((ts, LANES), lambda i: (i, 0))] * 2,
        out_shape=[jax.ShapeDtypeStruct((s, LANES), _F32)] * 2,
        compiler_params=_cparams("arbitrary"),
    )(positions.reshape(s // ROPE_PACK, ROPE_PACK), inv_row)


def _ada_pre(x, vec_ref):
    a = vec_ref[0:1, :] * (1.0 + vec_ref[2:3, :])
    return _rms(x) * a + vec_ref[1:2, :]


def _ffn_kernel(x_ref, vec_ref, win_ref, wout_ref, o_ref, h_ref, act_ref, *, tm):
    rows = [pl.ds(r * (tm // FFN_ROW_SPLIT), tm // FFN_ROW_SPLIT) for r in range(FFN_ROW_SPLIT)]
    tiles = FFN_DIM // MXU_DIM
    bounds = [MXU_DIM * (j * tiles // FFN_CHUNKS) for j in range(FFN_CHUNKS)] + [FFN_DIM]
    for r in rows:
        h_ref[r, :] = _ada_pre(x_ref[r, :], vec_ref).astype(_BF)
    for lo, hi in zip(bounds[:-1], bounds[1:]):
        for r in rows:
            g = _dot(h_ref[r, :], win_ref[:, lo:hi])
            u = _dot(h_ref[r, :], win_ref[:, FFN_DIM + lo:FFN_DIM + hi])
            act_ref[r, lo:hi] = (g * jax.nn.sigmoid(g) * u).astype(_BF)
    for r in rows:
        y = _dot(act_ref[r, :], wout_ref[...])
        o_ref[r, :] = x_ref[r, :] + (0.5 * vec_ref[3:4, :]) * (_rms(y) * vec_ref[4:5, :])


def _ffn_sublayer(x, vec, w_in, w_out):
    s, d = x.shape
    tm = min(FFN_ROW_TILE, s)
    return pl.pallas_call(
        functools.partial(_ffn_kernel, tm=tm),
        grid=(s // tm,),
        in_specs=[
            pl.BlockSpec((tm, d), lambda i: (i, 0)),
            _resident(vec.shape),
            _resident(w_in.shape),
            _resident(w_out.shape),
        ],
        out_specs=pl.BlockSpec((tm, d), lambda i: (i, 0)),
        out_shape=jax.ShapeDtypeStruct((s, d), _F32),
        scratch_shapes=[pltpu.VMEM((tm, d), _BF), pltpu.VMEM((tm, FFN_DIM), _BF)],
        compiler_params=_cparams("arbitrary"),
    )(x, vec, w_in, w_out)


def _swap_halves(t):
    n = t.shape[1]
    half = ATT_HEAD_DIM // 2
    from_right = pltpu.roll(t, n - half, 1)
    from_left = pltpu.roll(t, half, 1)
    lane = lax.broadcasted_iota(jnp.int32, t.shape, 1)
    return jnp.where((lane % ATT_HEAD_DIM) < half, from_right, from_left)


def _chunk_cumsums(g, tm):
    row = lax.broadcasted_iota(jnp.int32, g.shape, 0) % DN_CHUNK
    fwd, bwd = g, g
    sh = 1
    while sh < DN_CHUNK:
        fwd = fwd + jnp.where(row >= sh, pltpu.roll(fwd, sh, 0), 0.0)
        bwd = bwd + jnp.where(row < DN_CHUNK - sh, pltpu.roll(bwd, tm - sh, 0), 0.0)
        sh *= 2
    return fwd, bwd


def _store_by_class(val, out_ref, stage_ref, dil, tm):
    gw = ATT_GROUP_WIDTH
    if dil == 1:
        out_ref[...] = val.astype(_BF)
        return
    for c in range(gw // LANES):
        stage_ref[c] = val[:, c * LANES:(c + 1) * LANES]
    for r in range(dil):
        for c in range(gw // LANES):
            out_ref[:, r * gw + c * LANES:r * gw + (c + 1) * LANES] = (
                stage_ref[c, pl.ds(r, tm // dil, stride=dil), :].astype(_BF))


def _proj_kernel(xp_ref, x_ref, xn_ref, vec_ref, cos_ref, sin_ref, watt_ref, wdn_ref, wz_ref, wg_ref, wba_ref,
                 conv_ref, dec_ref,
                 q0_ref, q1_ref, q2_ref, k0_ref, k1_ref, k2_ref, v0_ref, v1_ref, v2_ref,
                 qd_ref, kd_ref, vd_ref, z_ref, gate_ref, tab_ref, tabt_ref,
                 h_ref, pdn_ref, dnst_ref, stage_ref, *, tm):
    i = pl.program_id(0)
    last = pl.num_programs(0) - 1
    cw = PROJ_CHUNK
    gw = ATT_GROUP_WIDTH
    x_ext = jnp.concatenate([xp_ref[0], x_ref[...], xn_ref[0]], axis=0)
    h_ref[...] = _ada_pre(x_ext, vec_ref).astype(_BF)
    inner = pl.ds(HALO, tm)
    cos = jnp.tile(cos_ref[...], (1, cw // LANES))
    sin = jnp.tile(sin_ref[...], (1, cw // LANES))
    row = lax.broadcasted_iota(jnp.int32, (tm + 2 * HALO, 1), 0)
    outside = ((i == 0) & (row < HALO)) | ((i == last) & (row >= HALO + tm))
    pad = DN_CONV // 2
    att_out = ((q0_ref, q1_ref, q2_ref), (k0_ref, k1_ref, k2_ref), (v0_ref, v1_ref, v2_ref))
    dn_out = (qd_ref, kd_ref, vd_ref)

    def att_task(kind, g):
        col = kind * ATT_WIDTH + g * gw

        def epilogue(t):
            if kind < 2:
                t = t * cos + _swap_halves(t) * sin
            if kind == 0:
                t = t * (ATT_HEAD_DIM ** -0.5)
            _store_by_class(t, att_out[kind][g], stage_ref.at[kind], ATT_GROUPS[g][1], tm)
        return (lambda: _dot(h_ref[inner, :], watt_ref[:, col:col + cw])), epilogue

    def dn_task(c):
        kind, col = divmod(c * cw, DN_WIDTH)
        slot = c % 2

        def epilogue(t):
            p = jnp.where(outside, 0.0, t)
            for hd in range(cw // DN_HEAD_DIM):
                hs = slice(hd * DN_HEAD_DIM, (hd + 1) * DN_HEAD_DIM)
                pdn_ref[slot, hd] = p[:, hs]
                wc = conv_ref[:, c * cw + hd * DN_HEAD_DIM:c * cw + (hd + 1) * DN_HEAD_DIM]
                for phase in range(2):
                    acc = None
                    for tap in range(DN_CONV):
                        rows = pdn_ref[slot, hd, pl.ds(HALO - pad + tap + phase, tm // 2, stride=2), :]
                        acc = rows * wc[tap:tap + 1, :] if acc is None else acc + rows * wc[tap:tap + 1, :]
                    t_h = acc * jax.nn.sigmoid(acc)
                    if kind < 2:
                        t_h = t_h * lax.rsqrt(jnp.sum(t_h * t_h, axis=-1, keepdims=True) + EPS)
                    if kind == 0:
                        t_h = t_h * (DN_HEAD_DIM ** -0.5)
                    dnst_ref[slot, hd, pl.ds(phase, tm // 2, stride=2), :] = t_h
                dn_out[kind][:, col + hd * DN_HEAD_DIM:col + (hd + 1) * DN_HEAD_DIM] = dnst_ref[slot, hd].astype(_BF)
        return (lambda: _dot(h_ref[...], wdn_ref[:, c * cw:(c + 1) * cw])), epilogue

    def z_task(c):
        def epilogue(t):
            z_ref[:, c * cw:(c + 1) * cw] = (t * jax.nn.sigmoid(t)).astype(_BF)
        return (lambda: _dot(h_ref[inner, :], wz_ref[:, c * cw:(c + 1) * cw])), epilogue

    def gate_task(c):
        def epilogue(t):
            gate_ref[:, c * cw:(c + 1) * cw] = jax.nn.sigmoid(t).astype(_BF)
        return (lambda: _dot(h_ref[inner, :], wg_ref[:, c * cw:(c + 1) * cw])), epilogue

    def table_task():
        def epilogue(raw):
            lane = lax.broadcasted_iota(jnp.int32, raw.shape, 1)
            beta = jax.nn.sigmoid(raw)
            log_decay = dec_ref[0:1, :] * jax.nn.softplus(raw + dec_ref[1:2, :])
            cum_f, cum_b = _chunk_cumsums(log_decay, tm)
            nh = DN_HEADS
            tab = jnp.where(lane < 2 * nh, beta, jnp.where(lane < 3 * nh, cum_f, jnp.where(lane < 4 * nh, cum_b, 0.0)))
            tab_ref[...] = tab
            tabt_ref[...] = tab.T[:4 * nh, :]
        return (lambda: _dot(h_ref[inner, :], wba_ref[...])), epilogue

    light = ([att_task(kind, g) for kind in range(3) for g in range(len(ATT_GROUPS))]
             + [z_task(c) for c in range(DN_WIDTH // cw)] + [gate_task(c) for c in range(2 * D_MODEL // cw)]
             + [table_task()])
    heavy = [dn_task(c) for c in range(3 * DN_WIDTH // cw)]
    tasks = []
    for n, task in enumerate(heavy):
        tasks += [task] + light[n * len(light) // len(heavy):(n + 1) * len(light) // len(heavy)]
    pending = None
    for matmul, epilogue in tasks:
        val = matmul()
        if pending is not None:
            pending[1](pending[0])
        pending = (val, epilogue)
    pending[1](pending[0])


def _mixer_projection(x, vec, cos_t, sin_t, wts, conv_w, dec):
    s, d = x.shape
    tm = min(ROW_TILE, s)
    nt = s // tm
    hb = tm // HALO
    x3 = x.reshape(s // HALO, HALO, d)
    row = lambda i: (i, 0)
    w_att, w_dn, w_z, w_g, w_ba = wts
    bf = lambda n: jax.ShapeDtypeStruct((s, n), _BF)
    gw = ATT_GROUP_WIDTH
    att_specs = [pl.BlockSpec((tm // dil, dil * gw), row) for _, dil in ATT_GROUPS] * 3
    att_shapes = [jax.ShapeDtypeStruct((s // dil, dil * gw), _BF) for _, dil in ATT_GROUPS] * 3
    outs = pl.pallas_call(
        functools.partial(_proj_kernel, tm=tm),
        grid=(nt,),
        in_specs=[
            pl.BlockSpec((1, HALO, d), lambda i: (jnp.maximum(i * hb - 1, 0), 0, 0)),
            pl.BlockSpec((tm, d), row),
            pl.BlockSpec((1, HALO, d), lambda i: (jnp.minimum((i + 1) * hb, s // HALO - 1), 0, 0)),
            _resident(vec.shape),
            pl.BlockSpec((tm, LANES), row),
            pl.BlockSpec((tm, LANES), row),
            _resident(w_att.shape), _resident(w_dn.shape), _resident(w_z.shape), _resident(w_g.shape),
            _resident(w_ba.shape), _resident(conv_w.shape), _resident(dec.shape),
        ],
        out_specs=att_specs + [
            pl.BlockSpec((tm, DN_WIDTH), row), pl.BlockSpec((tm, DN_WIDTH), row), pl.BlockSpec((tm, DN_WIDTH), row),
            pl.BlockSpec((tm, DN_WIDTH), row), pl.BlockSpec((tm, 2 * d), row),
            pl.BlockSpec((tm, LANES), row), pl.BlockSpec((4 * DN_HEADS, tm), lambda i: (0, i)),
        ],
        out_shape=att_shapes + [bf(DN_WIDTH), bf(DN_WIDTH), bf(DN_WIDTH), bf(DN_WIDTH), bf(2 * d),
                                jax.ShapeDtypeStruct((s, LANES), _F32), jax.ShapeDtypeStruct((4 * DN_HEADS, s), _F32)],
        scratch_shapes=[pltpu.VMEM((tm + 2 * HALO, d), _BF),
                        pltpu.VMEM((2, PROJ_CHUNK // DN_HEAD_DIM, tm + 2 * HALO, DN_HEAD_DIM), _F32),
                        pltpu.VMEM((2, PROJ_CHUNK // DN_HEAD_DIM, tm, DN_HEAD_DIM), _F32),
                        pltpu.VMEM((3, gw // LANES, tm, LANES), _F32)],
        compiler_params=_cparams("arbitrary"),
    )(x3, x, x3, vec, cos_t, sin_t, w_att, w_dn, w_z, w_g, w_ba, conv_w, dec)
    return outs[0:3], outs[3:6], outs[6:9], outs[9:]


def _att_kernel(q_ref, kp_ref, k_ref, kn_ref, vp_ref, v_ref, vn_ref, o_ref, lse_ref, *, nq, n_rows):
    i = pl.program_id(1)
    r = ATT_RADIUS
    qs = ATT_Q_SUB
    kw = qs + 2 * r
    nsub = nq // qs
    qi = lax.broadcasted_iota(jnp.int32, (2 * qs, kw), 0) % qs
    kj = lax.broadcasted_iota(jnp.int32, (2 * qs, kw), 1) - r
    band = jnp.where(jnp.abs(kj - qi) <= r, 0.0, NEG_INF)
    lane = lax.broadcasted_iota(jnp.int32, (1, LANES), 1)
    low = lane < ATT_HEAD_DIM
    kcol = lax.broadcasted_iota(jnp.int32, (1, kw), 1)

    def window(prev_ref, main_ref, next_ref, j, cs):
        lo, hi = j * qs - r, j * qs + qs + r
        parts = [prev_ref[:, cs]] if lo < 0 else []
        parts.append(main_ref[max(lo, 0):min(hi, nq), cs])
        if hi > nq:
            parts.append(next_ref[:, cs])
        return parts[0] if len(parts) == 1 else jnp.concatenate(parts, axis=0)

    for j in range(nsub):
        r0 = j * qs
        bias = band
        if j == 0 or j == nsub - 1:
            kidx = i * nq + r0 - r + kcol
            bias = band + jnp.where((kidx < 0) | (kidx >= n_rows), NEG_INF, 0.0)
        for hp in range(ATT_GROUP_WIDTH // LANES):
            cs = slice(hp * LANES, (hp + 1) * LANES)
            qp = q_ref[r0:r0 + qs, cs]
            kp = window(kp_ref, k_ref, kn_ref, j, cs)
            vp = window(vp_ref, v_ref, vn_ref, j, cs)
            zero = jnp.zeros_like(qp)
            qq = jnp.concatenate([jnp.where(low, qp, zero), jnp.where(low, zero, qp)], axis=0)
            sc = _dot_nt(qq, kp) + bias
            m = jnp.max(sc, axis=-1, keepdims=True)
            p = jnp.exp(sc - m)
            den = jnp.sum(p, axis=-1, keepdims=True)
            pv = _dot(p.astype(_BF), vp) / den
            lse = m + jnp.log(den)
            o_ref[r0:r0 + qs, cs] = jnp.where(low, pv[:qs], pv[qs:]).astype(_BF)
            lse_ref[r0:r0 + qs, cs] = jnp.where(low, lse[:qs], lse[qs:])


def _dilated_attention(qv, kv, vv, dilation):
    n_rows = qv.shape[0]
    nq = min(ATT_Q_TILE, n_rows)
    nt = n_rows // nq
    r = ATT_RADIUS
    gw = ATT_GROUP_WIDTH
    main = pl.BlockSpec((nq, gw), lambda c, i: (i, c))
    prev = pl.BlockSpec((r, gw), lambda c, i: (jnp.maximum(i * (nq // r) - 1, 0), c))
    nxt = pl.BlockSpec((r, gw), lambda c, i: (jnp.minimum((i + 1) * (nq // r), n_rows // r - 1), c))
    out = main
    return pl.pallas_call(
        functools.partial(_att_kernel, nq=nq, n_rows=n_rows),
        grid=(dilation, nt),
        in_specs=[main, prev, main, nxt, prev, main, nxt],
        out_specs=[out, out],
        out_shape=[jax.ShapeDtypeStruct((n_rows, dilation * gw), _BF),
                   jax.ShapeDtypeStruct((n_rows, dilation * gw), _F32)],
        compiler_params=_cparams("arbitrary", "arbitrary"),
    )(qv, kv, kv, kv, vv, vv, vv)


def _diag_blocks_by_lane(m):
    c = m.shape[0]
    b = DN_BASE
    blk = lax.broadcasted_iota(jnp.int32, (b, c), 1) // b
    dg = jnp.zeros((b, c), _F32)
    for bi in range(c // b):
        dg = jnp.where(blk == bi, m[bi * b:(bi + 1) * b, :], dg)
    return dg


def _blocks_to_diagonal(acc):
    b, c = acc.shape
    blk = lax.broadcasted_iota(jnp.int32, (b, c), 1) // b
    return jnp.concatenate([jnp.where(blk == bi, acc, 0.0) for bi in range(c // b)], axis=0)


def _pair_block_diag(t):
    c = t.shape[0]
    left = lax.broadcasted_iota(jnp.int32, (1, 2 * c), 1) < c
    zero = jnp.zeros_like(t)
    return jnp.concatenate([jnp.where(left, t, zero), jnp.where(left, zero, t)], axis=0)


def _dn_kernel(qf_ref, kf_ref, vf_ref, tf_ref, ttf_ref, qb_ref, kb_ref, vb_ref, tb_ref, ttb_ref,
               of_ref, ob_ref, state_ref, m_ref, attn_ref, x_ref, vbeta_ref, kbeta_ref, qg_ref, kdec_ref, gl_ref):
    step = pl.program_id(0)
    slots = (m_ref, attn_ref, x_ref, vbeta_ref, kbeta_ref, qg_ref, kdec_ref, gl_ref)

    @pl.when(step == 0)
    def _():
        state_ref[...] = jnp.zeros_like(state_ref)
        for ref in slots:
            ref[...] = jnp.zeros_like(ref)

    wr = step % 2
    rd = 1 - wr
    c = DN_CHUNK
    nh = DN_HEADS
    hd_w = DN_HEAD_DIM
    b = DN_BASE
    npairs = nh
    nsub = DN_STEP_CHUNKS
    ri = lax.broadcasted_iota(jnp.int32, (c, 2 * c), 0)
    lane2 = lax.broadcasted_iota(jnp.int32, (c, 2 * c), 1)
    ci = lane2 % c
    left = lane2 < c

    def pair_cols(tab, lane_a):
        return jnp.where(left, tab[:, lane_a:lane_a + 1], tab[:, lane_a + 1:lane_a + 2])

    def pair_slice(p):
        hd = (p % (nh // 2)) * 2
        return slice(hd * hd_w, (hd + 2) * hd_w)

    def table_lanes(p):
        lb = (p // (nh // 2)) * nh + (p % (nh // 2)) * 2
        return lb, 2 * nh + lb

    def prepare():
        ms = {}
        for sub in range(nsub):
            rows = pl.ds(sub * c, c)
            for direction, (q_ref, k_ref, v_ref, t_ref, tt_ref) in enumerate(
                    ((qf_ref, kf_ref, vf_ref, tf_ref, ttf_ref), (qb_ref, kb_ref, vb_ref, tb_ref, ttb_ref))):
                upper = direction == 1
                incl = (ri <= ci) if upper else (ri >= ci)
                strict = (ri < ci) if upper else (ri > ci)
                tab = t_ref[rows, :]
                tabt = tt_ref[:, sub * c:(sub + 1) * c]
                for p in range(direction * nh // 2, (direction + 1) * nh // 2):
                    cs = pair_slice(p)
                    lb, lg = table_lanes(p)
                    kp = k_ref[rows, cs]
                    beta = pair_cols(tab, lb)
                    gcol = pair_cols(tab, lg)
                    grow = jnp.concatenate([tabt[lg:lg + 1, :], tabt[lg + 1:lg + 2, :]], axis=1)
                    decay = jnp.exp(jnp.where(incl, gcol - grow, NEG_INF))
                    prod = _dot_nt(jnp.concatenate([q_ref[rows, cs], kp], axis=0), _pair_block_diag(kp))
                    m = jnp.where(strict, prod[c:] * beta * decay, 0.0)
                    m_ref[wr, sub, p] = m.astype(_BF)
                    attn_ref[wr, sub, p] = (prod[:c] * decay).astype(_BF)
                    ms[sub, p] = m
                    yield
                    k = kp.astype(_F32)
                    glast = gcol[0:1, :] if upper else gcol[c - 1:c, :]
                    eg = jnp.exp(gcol)
                    vbeta_ref[wr, sub, p] = (v_ref[rows, cs].astype(_F32) * beta).astype(_BF)
                    kbeta_ref[wr, sub, p] = (k * (beta * eg)).astype(_BF)
                    qg_ref[wr, sub, p] = (q_ref[rows, cs].astype(_F32) * eg).astype(_BF)
                    kdec_ref[wr, sub, p] = (k * jnp.exp(glast - gcol)).astype(_BF)
                    gl_ref[wr, sub, p] = jnp.broadcast_to(jnp.exp(glast), (SUBLANES, 2 * c))
                    yield
        keys = [[(sub, p) for sub in range(nsub) for p in range(d * nh // 2, (d + 1) * nh // 2)] for d in range(2)]
        dgs = [jnp.concatenate([_diag_blocks_by_lane(ms[key][:, h * c:(h + 1) * c]) for key in keys[d] for h in range(2)],
                               axis=0) for d in range(2)]
        nmat = 2 * len(keys[0])
        blk = lax.broadcasted_iota(jnp.int32, (nmat * b, c), 1) // b

        def column(dg, j):
            return jnp.take_along_axis(dg, blk * b + j, axis=1).reshape(nmat, b, c)

        lane = lax.broadcasted_iota(jnp.int32, (b, c), 1)
        sub_i = lax.broadcasted_iota(jnp.int32, (b, c), 0)
        eye = jnp.broadcast_to(jnp.where(sub_i == lane % b, 1.0, 0.0), (nmat, b, c))
        lo, up = eye, eye
        for t in range(b - 1):
            lo = lo - column(dgs[0], t) * lo[:, t:t + 1, :]
            up = up - column(dgs[1], b - 1 - t) * up[:, b - 1 - t:b - t, :]
            if t % 3 == 2:
                yield
        for d, acc in enumerate((lo, up)):
            for n, (sub, p) in enumerate(keys[d]):
                x_ref[wr, sub, p] = jnp.concatenate(
                    [_blocks_to_diagonal(acc[2 * n]), _blocks_to_diagonal(acc[2 * n + 1])], axis=1).astype(_BF)
            yield

    def recur(group, turn):
        at = {p: (turn if p < npairs // 2 else nsub - 1 - turn) for p in group}
        ms = {p: m_ref[rd, at[p], p] for p in group}
        xs = {p: x_ref[rd, at[p], p] for p in group}
        size = b
        one = jnp.ones((), _BF)
        while size < c:
            off_diag = (ri // (2 * size) == ci // (2 * size)) & (ri // size != ci // size)
            zero = jnp.zeros_like(ms[group[0]])
            ts = {p: _dot(xs[p], _pair_block_diag(jnp.where(off_diag, ms[p], zero))).astype(_BF) for p in group}
            yield
            xs = {p: _dot(jnp.where(ri == ci, one, -ts[p]), _pair_block_diag(xs[p])).astype(_BF) for p in group}
            yield
            size *= 2
        us, wqs = {}, {}
        for p in group:
            vbeta, kbeta = vbeta_ref[rd, at[p], p], kbeta_ref[rd, at[p], p]
            sols = [_dot(xs[p][:, h * c:(h + 1) * c],
                         jnp.concatenate([vbeta[:, h * hd_w:(h + 1) * hd_w], kbeta[:, h * hd_w:(h + 1) * hd_w]], axis=1))
                    for h in range(2)]
            us[p] = jnp.concatenate([sols[0][:, :hd_w], sols[1][:, :hd_w]], axis=1)
            w = jnp.concatenate([sols[0][:, hd_w:], sols[1][:, hd_w:]], axis=1)
            wqs[p] = jnp.concatenate([w.astype(_BF), qg_ref[rd, at[p], p]], axis=0)
            yield
        states = {p: state_ref[p] for p in group}
        wss = {p: _dot(wqs[p], _pair_block_diag(states[p].astype(_BF))) for p in group}
        yield
        vbs = {p: (us[p] - wss[p][:c]).astype(_BF) for p in group}
        for p in group:
            o_ref = of_ref if p < npairs // 2 else ob_ref
            o_ref[pl.ds(at[p] * c, c), pair_slice(p)] = (
                wss[p][c:] + _dot(attn_ref[rd, at[p], p], _pair_block_diag(vbs[p])))
        yield
        for p in group:
            kdec = kdec_ref[rd, at[p], p]
            upd = [_dot_tn(kdec[:, h * hd_w:(h + 1) * hd_w], vbs[p][:, h * hd_w:(h + 1) * hd_w]) for h in range(2)]
            state_ref[p] = states[p] * gl_ref[rd, at[p], p, 0:1, :] + jnp.concatenate(upd, axis=1)
        yield

    groups = [list(range(g, g + DN_GROUP)) for g in range(0, npairs, DN_GROUP)]
    recurrence = itertools.chain(*(recur(g, turn) for turn in range(nsub) for g in groups))
    for _ in itertools.zip_longest(recurrence, prepare()):
        pass


def _gated_delta(qd, kd, vd, tab, tabt):
    s = qd.shape[0]
    c = DN_CHUNK
    nsub = DN_STEP_CHUNKS
    rows = nsub * c
    n = s // rows
    npairs = DN_HEADS
    def block_of(stream, flip):
        def index(i):
            j = jnp.maximum(i - 1, 0) if stream == "rec" else jnp.minimum(i, n - 1)
            return n - 1 - j if flip else j
        return index

    def wide(stream, flip):
        index = block_of(stream, flip)
        return pl.BlockSpec((rows, DN_WIDTH), lambda i: (index(i), 0))

    def table(flip):
        index = block_of("prep", flip)
        return pl.BlockSpec((rows, LANES), lambda i: (index(i), 0))

    def table_t(flip):
        index = block_of("prep", flip)
        return pl.BlockSpec((4 * DN_HEADS, rows), lambda i: (0, index(i)))

    specs = []
    for flip in (False, True):
        specs += [wide("prep", flip), wide("prep", flip), wide("prep", flip), table(flip), table_t(flip)]
    slot = pltpu.VMEM((2, nsub, npairs, c, 2 * c), _BF)
    return pl.pallas_call(
        _dn_kernel,
        grid=(n + 1,),
        in_specs=specs,
        out_specs=[wide("rec", False), wide("rec", True)],
        out_shape=[jax.ShapeDtypeStruct((s, DN_WIDTH), _F32)] * 2,
        scratch_shapes=[pltpu.VMEM((npairs, DN_HEAD_DIM, 2 * DN_HEAD_DIM), _F32),
                        slot, slot, slot,
                        slot, slot, slot, slot,
                        pltpu.VMEM((2, nsub, npairs, SUBLANES, 2 * c), _F32)],
        compiler_params=_cparams("arbitrary"),
    )(qd, kd, vd, tab, tabt, qd, kd, vd, tab, tabt)


def _token_order(blk_ref, stage_ref, dil, row0, rows):
    gw = ATT_GROUP_WIDTH
    src = pl.ds(row0 // dil, rows // dil)
    if dil == 1:
        return blk_ref[src, :].astype(_F32)
    for r in range(dil):
        for c in range(gw // LANES):
            stage_ref[c, pl.ds(row0 + r, rows // dil, stride=dil), :] = (
                blk_ref[src, r * gw + c * LANES:r * gw + (c + 1) * LANES].astype(_F32))
    return jnp.concatenate([stage_ref[c, pl.ds(row0, rows), :] for c in range(gw // LANES)], axis=1)


def _out_kernel(x_ref, o0_ref, o1_ref, o2_ref, l0_ref, l1_ref, l2_ref, of_ref, ob_ref, z_ref, gate_ref, vec_ref,
                wa_ref, wd_ref, wo_ref, out_ref, oatt_ref, odn_ref, y_ref, stage_ref, *, tm):
    d = D_MODEL
    dils = [dil for _, dil in ATT_GROUPS]
    rows = tm // OUT_ROW_SPLIT
    subs = [(r * rows, pl.ds(r * rows, rows)) for r in range(OUT_ROW_SPLIT)]
    for row0, rs in subs:
        o0, o1, o2 = (_token_order(ref, stage_ref.at[n], dils[n], row0, rows)
                      for n, ref in enumerate((o0_ref, o1_ref, o2_ref)))
        l0, l1, l2 = (_token_order(ref, stage_ref.at[3 + n], dils[n], row0, rows)
                      for n, ref in enumerate((l0_ref, l1_ref, l2_ref)))
        mx = jnp.maximum(jnp.maximum(l0, l1), l2)
        e0, e1, e2 = jnp.exp(l0 - mx), jnp.exp(l1 - mx), jnp.exp(l2 - mx)
        oatt_ref[rs, :] = ((e0 * o0 + e1 * o1 + e2 * o2) / (e0 + e1 + e2)).astype(_BF)
        for hd in range(DN_HEADS):
            sl = slice(hd * DN_HEAD_DIM, (hd + 1) * DN_HEAD_DIM)
            o = of_ref[rs, sl] + ob_ref[rs, sl]
            o = _rms(o) * vec_ref[5:6, 0:DN_HEAD_DIM]
            odn_ref[rs, sl] = (o * z_ref[rs, sl].astype(_F32)).astype(_BF)
    for row0, rs in subs:
        ya = _dot(oatt_ref[rs, :], wa_ref[...])
        yd = _dot(odn_ref[rs, :], wd_ref[...])
        y_ref[rs, :] = (gate_ref[rs, :d].astype(_F32) * ya + gate_ref[rs, d:].astype(_F32) * yd).astype(_BF)
    for row0, rs in subs:
        y = _dot(y_ref[rs, :], wo_ref[...])
        out_ref[rs, :] = x_ref[rs, :] + vec_ref[3:4, :] * (_rms(y) * vec_ref[4:5, :])


def _mixer_output(x, att, o_f, o_b, z, gates, vec, w_a, w_d, w_o):
    s, d = x.shape
    tm = min(ROW_TILE, s)
    row = lambda i: (i, 0)
    (o0, l0), (o1, l1), (o2, l2) = att
    gw = ATT_GROUP_WIDTH
    by_class = [pl.BlockSpec((tm // dil, dil * gw), row) for _, dil in ATT_GROUPS]
    return pl.pallas_call(
        functools.partial(_out_kernel, tm=tm),
        grid=(s // tm,),
        in_specs=[pl.BlockSpec((tm, d), row)] + by_class * 2
        + [pl.BlockSpec((tm, DN_WIDTH), row)] * 3 + [pl.BlockSpec((tm, 2 * d), row)]
        + [_resident(vec.shape), _resident(w_a.shape), _resident(w_d.shape), _resident(w_o.shape)],
        out_specs=pl.BlockSpec((tm, d), row),
        out_shape=jax.ShapeDtypeStruct((s, d), _F32),
        scratch_shapes=[pltpu.VMEM((tm, gw), _BF), pltpu.VMEM((tm, DN_WIDTH), _BF), pltpu.VMEM((tm, d), _BF),
                        pltpu.VMEM((6, gw // LANES, tm, LANES), _F32)],
        compiler_params=_cparams("arbitrary"),
    )(x, o0, o1, o2, l0, l1, l2, o_f, o_b, z, gates, vec, w_a, w_d, w_o)


def _vec(rows, d):
    pad = [jnp.zeros((d,), _F32)] * (SUBLANES - len(rows))
    return jnp.stack(list(rows) + pad, axis=0)


def kernel(x, c, positions, w_ada, b_ada, norm_pre, norm_post, ffn_w_in, ffn_w_out, w_in, conv_w, a_log, dt_bias,
           dn_norm_w, w_branch_att, w_branch_dn, w_out):
    b, s, d = x.shape
    assert b == 1 and d == D_MODEL and s % (DN_CHUNK * 16) == 0
    depth = w_ada.shape[0]
    x2 = x.reshape(s, d)
    mod = _ada_modulation(c, w_ada, b_ada)
    cos_t, sin_t = _rope_tables(positions)

    o_dn = 3 * ATT_WIDTH
    o_z = o_dn + 3 * DN_WIDTH
    o_ba = o_z + DN_WIDTH
    o_g = o_ba + 4 * DN_HEADS
    nh = DN_HEADS
    zeros_row = jnp.zeros((LANES,), _F32)

    for l in range(depth):
        def vec_for(sub, extra=None):
            rows = [norm_pre[l, sub], mod[l, sub, 0], mod[l, sub, 1], mod[l, sub, 2], norm_post[l, sub]]
            if extra is not None:
                rows.append(extra)
            return _vec(rows, d)

        x2 = _ffn_sublayer(x2, vec_for(0), ffn_w_in[l, 0].astype(_BF), ffn_w_out[l, 0].astype(_BF))

        wl = w_in[l]
        w_ba = jnp.pad(wl[:, o_ba:o_g], ((0, 0), (0, LANES - 4 * nh))).astype(_BF)
        wts = (wl[:, :o_dn].astype(_BF), wl[:, o_dn:o_z].astype(_BF), wl[:, o_z:o_ba].astype(_BF),
               wl[:, o_g:].astype(_BF), w_ba)
        neg_a = zeros_row.at[2 * nh:4 * nh].set(-jnp.exp(a_log[l].astype(_F32)).reshape(-1))
        dtb = zeros_row.at[2 * nh:4 * nh].set(dt_bias[l].astype(_F32).reshape(-1))
        dec = _vec([neg_a, dtb], LANES)
        qa, ka, va, (qd, kd, vd, z, gates, tab, tabt) = _mixer_projection(
            x2, vec_for(1), cos_t, sin_t, wts, conv_w[l], dec)
        att = [_dilated_attention(qa[g], ka[g], va[g], dil) for g, (_, dil) in enumerate(ATT_GROUPS)]
        o_f, o_b = _gated_delta(qd, kd, vd, tab, tabt)
        x2 = _mixer_output(x2, att, o_f, o_b, z, gates, vec_for(1, jnp.tile(dn_norm_w[l], d // DN_HEAD_DIM)),
                           w_branch_att[l].astype(_BF), w_branch_dn[l].astype(_BF), w_out[l].astype(_BF))

        x2 = _ffn_sublayer(x2, vec_for(2), ffn_w_in[l, 1].astype(_BF), ffn_w_out[l, 1].astype(_BF))
    return x2.reshape(b, s, d)
```

```python
import functools
import itertools

import jax
import jax.numpy as jnp
from jax import lax
from jax.experimental import pallas as pl
from jax.experimental.pallas import tpu as pltpu

D_MODEL = 1024
DEPTH = 2
N_SUBLAYERS = 3
ATT_GROUPS = ((128, 1), (512, 4), (2048, 16))
ATT_HEADS_PER_GROUP = 4
ATT_HEAD_DIM = 64
ATT_WIDTH = len(ATT_GROUPS) * ATT_HEADS_PER_GROUP * ATT_HEAD_DIM
ATT_GROUP_WIDTH = ATT_HEADS_PER_GROUP * ATT_HEAD_DIM
ATT_RADIUS = 64
ROPE_THETA = 10000.0
DN_HEADS = 6
DN_HEAD_DIM = 128
DN_WIDTH = DN_HEADS * DN_HEAD_DIM
DN_CONV = 5
FFN_DIM = 2816
EPS = 1e-6
NEG_INF = -1e30

LANES = 128
SUBLANES = 8
BF16_ROWS = 16
MXU_DIM = 256
VMEM_LIMIT_BYTES = 56 * 1024 * 1024

ROPE_PACK = LANES // (ATT_HEAD_DIM // 2)
ADA_ROW_TILE = 256
ROW_TILE = 512
FFN_CHUNKS = 2
FFN_ROW_TILE = 1024
FFN_ROW_SPLIT = 4
OUT_ROW_SPLIT = 2
PROJ_CHUNK = 256
ATT_Q_TILE = 1024
ATT_Q_SUB = 128
DN_CHUNK = 128
DN_BASE = 16
DN_STEP_CHUNKS = 2
DN_GROUP = 6
HALO = BF16_ROWS

_BF = jnp.bfloat16
_F32 = jnp.float32


def _cparams(*sem):
    return pltpu.CompilerParams(dimension_semantics=sem, vmem_limit_bytes=VMEM_LIMIT_BYTES)


def _resident(shape):
    nd = len(shape)
    return pl.BlockSpec(shape, lambda *_: (0,) * nd, pipeline_mode=pl.Buffered(1))


def _rms(t):
    return t * lax.rsqrt(jnp.mean(t * t, axis=-1, keepdims=True) + EPS)


def _dot(a, b):
    return jnp.dot(a, b, preferred_element_type=_F32)


def _dot_nt(a, b):
    return lax.dot_general(a, b, (((1,), (1,)), ((), ())), preferred_element_type=_F32)


def _dot_tn(a, b):
    return lax.dot_general(a, b, (((0,), (0,)), ((), ())), preferred_element_type=_F32)


def _ada_kernel(c_ref, w_ref, b_ref, o_ref):
    c = c_ref[...]
    s = c * jax.nn.sigmoid(c)
    part = jnp.sum(w_ref[0] * s, axis=0, keepdims=True)

    @pl.when(pl.program_id(1) == 0)
    def _():
        o_ref[0] = b_ref[0] + part

    @pl.when(pl.program_id(1) != 0)
    def _():
        o_ref[0] = o_ref[0] + part


def _ada_modulation(c, w_ada, b_ada):
    depth, d, n = w_ada.shape
    tk = ADA_ROW_TILE
    out = pl.pallas_call(
        _ada_kernel,
        grid=(depth, d // tk),
        in_specs=[
            pl.BlockSpec((tk, 1), lambda l, k: (k, 0)),
            pl.BlockSpec((1, tk, n), lambda l, k: (l, k, 0)),
            pl.BlockSpec((1, 1, n), lambda l, k: (l, 0, 0)),
        ],
        out_specs=pl.BlockSpec((1, 1, n), lambda l, k: (l, 0, 0)),
        out_shape=jax.ShapeDtypeStruct((depth, 1, n), _F32),
        compiler_params=_cparams("arbitrary", "arbitrary"),
    )(c.reshape(d, 1), w_ada, b_ada.reshape(depth, 1, n))
    return out.reshape(depth, N_SUBLAYERS, 3, d)


def _rope_kernel(pos_ref, inv_ref, cos_ref, sin_ref):
    half = ATT_HEAD_DIM // 2
    pos = pos_ref[...].astype(_F32)
    rows = pos.shape[0]
    lane = lax.broadcasted_iota(jnp.int32, (rows, LANES), 1)
    grp = lane // half
    packed = pos[:, ROPE_PACK - 1:ROPE_PACK]
    for j in range(ROPE_PACK - 2, -1, -1):
        packed = jnp.where(grp == j, pos[:, j:j + 1], packed)
    ang = packed * inv_ref[...]
    first_half = (lane % ATT_HEAD_DIM) < half
    for val, out_ref, signed in ((jnp.cos(ang), cos_ref, False), (jnp.sin(ang), sin_ref, True)):
        for j in range(ROPE_PACK):
            y = jnp.where(grp == j, val, 0.0)
            shift = half
            while shift < LANES:
                y = y + pltpu.roll(y, shift, 1)
                shift *= 2
            if signed:
                y = jnp.where(first_half, -y, y)
            out_ref[pl.ds(j, rows, stride=ROPE_PACK), :] = y


def _rope_tables(positions):
    s = positions.shape[1]
    ts = min(2048, s)
    half = ATT_HEAD_DIM // 2
    inv = ROPE_THETA ** (-jnp.arange(half, dtype=_F32) * 2.0 / ATT_HEAD_DIM)
    inv_row = jnp.tile(inv, LANES // half).reshape(1, LANES)
    return pl.pallas_call(
        _rope_kernel,
        grid=(s // ts,),
        in_specs=[pl.BlockSpec((ts // ROPE_PACK, ROPE_PACK), lambda i: (i, 0)),
                  pl.BlockSpec((1, LANES), lambda i: (0, 0))],
        out_specs=[pl.BlockSpec((ts, LANES), lambda i: (i, 0))] * 2,
        out_shape=[jax.ShapeDtypeStruct((s, LANES), _F32)] * 2,
        compiler_params=_cparams("arbitrary"),
    )(positions.reshape(s // ROPE_PACK, ROPE_PACK), inv_row)


def _ada_pre(x, vec_ref):
    a = vec_ref[0:1, :] * (1.0 + vec_ref[2:3, :])
    return _rms(x) * a + vec_ref[1:2, :]


def _ffn_kernel(x_ref, vec_ref, win_ref, wout_ref, o_ref, h_ref, act_ref, *, tm):
    rows = [pl.ds(r * (tm // FFN_ROW_SPLIT), tm // FFN_ROW_SPLIT) for r in range(FFN_ROW_SPLIT)]
    tiles = FFN_DIM // MXU_DIM
    bounds = [MXU_DIM * (j * tiles // FFN_CHUNKS) for j in range(FFN_CHUNKS)] + [FFN_DIM]
    for r in rows:
        h_ref[r, :] = _ada_pre(x_ref[r, :], vec_ref).astype(_BF)
    for lo, hi in zip(bounds[:-1], bounds[1:]):
        for r in rows:
            g = _dot(h_ref[r, :], win_ref[:, lo:hi])
            u = _dot(h_ref[r, :], win_ref[:, FFN_DIM + lo:FFN_DIM + hi])
            act_ref[r, lo:hi] = (g * jax.nn.sigmoid(g) * u).astype(_BF)
    for r in rows:
        y = _dot(act_ref[r, :], wout_ref[...])
        o_ref[r, :] = x_ref[r, :] + (0.5 * vec_ref[3:4, :]) * (_rms(y) * vec_ref[4:5, :])


def _ffn_sublayer(x, vec, w_in, w_out):
    s, d = x.shape
    tm = min(FFN_ROW_TILE, s)
    return pl.pallas_call(
        functools.partial(_ffn_kernel, tm=tm),
        grid=(s // tm,),
        in_specs=[
            pl.BlockSpec((tm, d), lambda i: (i, 0)),
            _resident(vec.shape),
            _resident(w_in.shape),
            _resident(w_out.shape),
        ],
        out_specs=pl.BlockSpec((tm, d), lambda i: (i, 0)),
        out_shape=jax.ShapeDtypeStruct((s, d), _F32),
        scratch_shapes=[pltpu.VMEM((tm, d), _BF), pltpu.VMEM((tm, FFN_DIM), _BF)],
        compiler_params=_cparams("arbitrary"),
    )(x, vec, w_in, w_out)


def _swap_halves(t):
    n = t.shape[1]
    half = ATT_HEAD_DIM // 2
    from_right = pltpu.roll(t, n - half, 1)
    from_left = pltpu.roll(t, half, 1)
    lane = lax.broadcasted_iota(jnp.int32, t.shape, 1)
    return jnp.where((lane % ATT_HEAD_DIM) < half, from_right, from_left)


def _chunk_cumsums(g, tm):
    row = lax.broadcasted_iota(jnp.int32, g.shape, 0) % DN_CHUNK
    fwd, bwd = g, g
    sh = 1
    while sh < DN_CHUNK:
        fwd = fwd + jnp.where(row >= sh, pltpu.roll(fwd, sh, 0), 0.0)
        bwd = bwd + jnp.where(row < DN_CHUNK - sh, pltpu.roll(bwd, tm - sh, 0), 0.0)
        sh *= 2
    return fwd, bwd


def _store_by_class(val, out_ref, stage_ref, dil, tm):
    gw = ATT_GROUP_WIDTH
    if dil == 1:
        out_ref[...] = val.astype(_BF)
        return
    for c in range(gw // LANES):
        stage_ref[c] = val[:, c * LANES:(c + 1) * LANES]
    for r in range(dil):
        for c in range(gw // LANES):
            out_ref[:, r * gw + c * LANES:r * gw + (c + 1) * LANES] = (
                stage_ref[c, pl.ds(r, tm // dil, stride=dil), :].astype(_BF))


def _proj_kernel(xp_ref, x_ref, xn_ref, vec_ref, cos_ref, sin_ref, watt_ref, wdn_ref, wz_ref, wg_ref, wba_ref,
                 conv_ref, dec_ref,
                 q0_ref, q1_ref, q2_ref, k0_ref, k1_ref, k2_ref, v0_ref, v1_ref, v2_ref,
                 qd_ref, kd_ref, vd_ref, z_ref, gate_ref, tab_ref, tabt_ref,
                 h_ref, pdn_ref, dnst_ref, stage_ref, *, tm):
    i = pl.program_id(0)
    last = pl.num_programs(0) - 1
    cw = PROJ_CHUNK
    gw = ATT_GROUP_WIDTH
    x_ext = jnp.concatenate([xp_ref[0], x_ref[...], xn_ref[0]], axis=0)
    h_ref[...] = _ada_pre(x_ext, vec_ref).astype(_BF)
    inner = pl.ds(HALO, tm)
    cos = jnp.tile(cos_ref[...], (1, cw // LANES))
    sin = jnp.tile(sin_ref[...], (1, cw // LANES))
    row = lax.broadcasted_iota(jnp.int32, (tm + 2 * HALO, 1), 0)
    outside = ((i == 0) & (row < HALO)) | ((i == last) & (row >= HALO + tm))
    pad = DN_CONV // 2
    att_out = ((q0_ref, q1_ref, q2_ref), (k0_ref, k1_ref, k2_ref), (v0_ref, v1_ref, v2_ref))
    dn_out = (qd_ref, kd_ref, vd_ref)

    def att_task(kind, g):
        col = kind * ATT_WIDTH + g * gw

        def epilogue(t):
            if kind < 2:
                t = t * cos + _swap_halves(t) * sin
            if kind == 0:
                t = t * (ATT_HEAD_DIM ** -0.5)
            _store_by_class(t, att_out[kind][g], stage_ref.at[kind], ATT_GROUPS[g][1], tm)
        return (lambda: _dot(h_ref[inner, :], watt_ref[:, col:col + cw])), epilogue

    def dn_task(c):
        kind, col = divmod(c * cw, DN_WIDTH)
        slot = c % 2

        def epilogue(t):
            p = jnp.where(outside, 0.0, t)
            for hd in range(cw // DN_HEAD_DIM):
                hs = slice(hd * DN_HEAD_DIM, (hd + 1) * DN_HEAD_DIM)
                pdn_ref[slot, hd] = p[:, hs]
                wc = conv_ref[:, c * cw + hd * DN_HEAD_DIM:c * cw + (hd + 1) * DN_HEAD_DIM]
                for phase in range(2):
                    acc = None
                    for tap in range(DN_CONV):
                        rows = pdn_ref[slot, hd, pl.ds(HALO - pad + tap + phase, tm // 2, stride=2), :]
                        acc = rows * wc[tap:tap + 1, :] if acc is None else acc + rows * wc[tap:tap + 1, :]
                    t_h = acc * jax.nn.sigmoid(acc)
                    if kind < 2:
                        t_h = t_h * lax.rsqrt(jnp.sum(t_h * t_h, axis=-1, keepdims=True) + EPS)
                    if kind == 0:
                        t_h = t_h * (DN_HEAD_DIM ** -0.5)
                    dnst_ref[slot, hd, pl.ds(phase, tm // 2, stride=2), :] = t_h
                dn_out[kind][:, col + hd * DN_HEAD_DIM:col + (hd + 1) * DN_HEAD_DIM] = dnst_ref[slot, hd].astype(_BF)
        return (lambda: _dot(h_ref[...], wdn_ref[:, c * cw:(c + 1) * cw])), epilogue

    def z_task(c):
        def epilogue(t):
            z_ref[:, c * cw:(c + 1) * cw] = (t * jax.nn.sigmoid(t)).astype(_BF)
        return (lambda: _dot(h_ref[inner, :], wz_ref[:, c * cw:(c + 1) * cw])), epilogue

    def gate_task(c):
        def epilogue(t):
            gate_ref[:, c * cw:(c + 1) * cw] = jax.nn.sigmoid(t).astype(_BF)
        return (lambda: _dot(h_ref[inner, :], wg_ref[:, c * cw:(c + 1) * cw])), epilogue

    def table_task():
        def epilogue(raw):
            lane = lax.broadcasted_iota(jnp.int32, raw.shape, 1)
            beta = jax.nn.sigmoid(raw)
            log_decay = dec_ref[0:1, :] * jax.nn.softplus(raw + dec_ref[1:2, :])
            cum_f, cum_b = _chunk_cumsums(log_decay, tm)
            nh = DN_HEADS
            tab = jnp.where(lane < 2 * nh, beta, jnp.where(lane < 3 * nh, cum_f, jnp.where(lane < 4 * nh, cum_b, 0.0)))
            tab_ref[...] = tab
            tabt_ref[...] = tab.T[:4 * nh, :]
        return (lambda: _dot(h_ref[inner, :], wba_ref[...])), epilogue

    light = ([att_task(kind, g) for kind in range(3) for g in range(len(ATT_GROUPS))]
             + [z_task(c) for c in range(DN_WIDTH // cw)] + [gate_task(c) for c in range(2 * D_MODEL // cw)]
             + [table_task()])
    heavy = [dn_task(c) for c in range(3 * DN_WIDTH // cw)]
    tasks = []
    for n, task in enumerate(heavy):
        tasks += [task] + light[n * len(light) // len(heavy):(n + 1) * len(light) // len(heavy)]
    pending = None
    for matmul, epilogue in tasks:
        val = matmul()
        if pending is not None:
            pending[1](pending[0])
        pending = (val, epilogue)
    pending[1](pending[0])


def _mixer_projection(x, vec, cos_t, sin_t, wts, conv_w, dec):
    s, d = x.shape
    tm = min(ROW_TILE, s)
    nt = s // tm
    hb = tm // HALO
    x3 = x.reshape(s // HALO, HALO, d)
    row = lambda i: (i, 0)
    w_att, w_dn, w_z, w_g, w_ba = wts
    bf = lambda n: jax.ShapeDtypeStruct((s, n), _BF)
    gw = ATT_GROUP_WIDTH
    att_specs = [pl.BlockSpec((tm // dil, dil * gw), row) for _, dil in ATT_GROUPS] * 3
    att_shapes = [jax.ShapeDtypeStruct((s // dil, dil * gw), _BF) for _, dil in ATT_GROUPS] * 3
    outs = pl.pallas_call(
        functools.partial(_proj_kernel, tm=tm),
        grid=(nt,),
        in_specs=[
            pl.BlockSpec((1, HALO, d), lambda i: (jnp.maximum(i * hb - 1, 0), 0, 0)),
            pl.BlockSpec((tm, d), row),
            pl.BlockSpec((1, HALO, d), lambda i: (jnp.minimum((i + 1) * hb, s // HALO - 1), 0, 0)),
            _resident(vec.shape),
            pl.BlockSpec((tm, LANES), row),
            pl.BlockSpec((tm, LANES), row),
            _resident(w_att.shape), _resident(w_dn.shape), _resident(w_z.shape), _resident(w_g.shape),
            _resident(w_ba.shape), _resident(conv_w.shape), _resident(dec.shape),
        ],
        out_specs=att_specs + [
            pl.BlockSpec((tm, DN_WIDTH), row), pl.BlockSpec((tm, DN_WIDTH), row), pl.BlockSpec((tm, DN_WIDTH), row),
            pl.BlockSpec((tm, DN_WIDTH), row), pl.BlockSpec((tm, 2 * d), row),
            pl.BlockSpec((tm, LANES), row), pl.BlockSpec((4 * DN_HEADS, tm), lambda i: (0, i)),
        ],
        out_shape=att_shapes + [bf(DN_WIDTH), bf(DN_WIDTH), bf(DN_WIDTH), bf(DN_WIDTH), bf(2 * d),
                                jax.ShapeDtypeStruct((s, LANES), _F32), jax.ShapeDtypeStruct((4 * DN_HEADS, s), _F32)],
        scratch_shapes=[pltpu.VMEM((tm + 2 * HALO, d), _BF),
                        pltpu.VMEM((2, PROJ_CHUNK // DN_HEAD_DIM, tm + 2 * HALO, DN_HEAD_DIM), _F32),
                        pltpu.VMEM((2, PROJ_CHUNK // DN_HEAD_DIM, tm, DN_HEAD_DIM), _F32),
                        pltpu.VMEM((3, gw // LANES, tm, LANES), _F32)],
        compiler_params=_cparams("arbitrary"),
    )(x3, x, x3, vec, cos_t, sin_t, w_att, w_dn, w_z, w_g, w_ba, conv_w, dec)
    return outs[0:3], outs[3:6], outs[6:9], outs[9:]


def _att_kernel(q_ref, kp_ref, k_ref, kn_ref, vp_ref, v_ref, vn_ref, o_ref, lse_ref, *, nq, n_rows):
    i = pl.program_id(1)
    r = ATT_RADIUS
    qs = ATT_Q_SUB
    kw = qs + 2 * r
    nsub = nq // qs
    qi = lax.broadcasted_iota(jnp.int32, (2 * qs, kw), 0) % qs
    kj = lax.broadcasted_iota(jnp.int32, (2 * qs, kw), 1) - r
    band = jnp.where(jnp.abs(kj - qi) <= r, 0.0, NEG_INF)
    lane = lax.broadcasted_iota(jnp.int32, (1, LANES), 1)
    low = lane < ATT_HEAD_DIM
    kcol = lax.broadcasted_iota(jnp.int32, (1, kw), 1)

    def window(prev_ref, main_ref, next_ref, j, cs):
        lo, hi = j * qs - r, j * qs + qs + r
        parts = [prev_ref[:, cs]] if lo < 0 else []
        parts.append(main_ref[max(lo, 0):min(hi, nq), cs])
        if hi > nq:
            parts.append(next_ref[:, cs])
        return parts[0] if len(parts) == 1 else jnp.concatenate(parts, axis=0)

    for j in range(nsub):
        r0 = j * qs
        bias = band
        if j == 0 or j == nsub - 1:
            kidx = i * nq + r0 - r + kcol
            bias = band + jnp.where((kidx < 0) | (kidx >= n_rows), NEG_INF, 0.0)
        for hp in range(ATT_GROUP_WIDTH // LANES):
            cs = slice(hp * LANES, (hp + 1) * LANES)
            qp = q_ref[r0:r0 + qs, cs]
            kp = window(kp_ref, k_ref, kn_ref, j, cs)
            vp = window(vp_ref, v_ref, vn_ref, j, cs)
            zero = jnp.zeros_like(qp)
            qq = jnp.concatenate([jnp.where(low, qp, zero), jnp.where(low, zero, qp)], axis=0)
            sc = _dot_nt(qq, kp) + bias
            m = jnp.max(sc, axis=-1, keepdims=True)
            p = jnp.exp(sc - m)
            den = jnp.sum(p, axis=-1, keepdims=True)
            pv = _dot(p.astype(_BF), vp) / den
            lse = m + jnp.log(den)
            o_ref[r0:r0 + qs, cs] = jnp.where(low, pv[:qs], pv[qs:]).astype(_BF)
            lse_ref[r0:r0 + qs, cs] = jnp.where(low, lse[:qs], lse[qs:])


def _dilated_attention(qv, kv, vv, dilation):
    n_rows = qv.shape[0]
    nq = min(ATT_Q_TILE, n_rows)
    nt = n_rows // nq
    r = ATT_RADIUS
    gw = ATT_GROUP_WIDTH
    main = pl.BlockSpec((nq, gw), lambda c, i: (i, c))
    prev = pl.BlockSpec((r, gw), lambda c, i: (jnp.maximum(i * (nq // r) - 1, 0), c))
    nxt = pl.BlockSpec((r, gw), lambda c, i: (jnp.minimum((i + 1) * (nq // r), n_rows // r - 1), c))
    out = main
    return pl.pallas_call(
        functools.partial(_att_kernel, nq=nq, n_rows=n_rows),
        grid=(dilation, nt),
        in_specs=[main, prev, main, nxt, prev, main, nxt],
        out_specs=[out, out],
        out_shape=[jax.ShapeDtypeStruct((n_rows, dilation * gw), _BF),
                   jax.ShapeDtypeStruct((n_rows, dilation * gw), _F32)],
        compiler_params=_cparams("arbitrary", "arbitrary"),
    )(qv, kv, kv, kv, vv, vv, vv)


def _diag_blocks_by_lane(m):
    c = m.shape[0]
    b = DN_BASE
    blk = lax.broadcasted_iota(jnp.int32, (b, c), 1) // b
    dg = jnp.zeros((b, c), _F32)
    for bi in range(c // b):
        dg = jnp.where(blk == bi, m[bi * b:(bi + 1) * b, :], dg)
    return dg


def _blocks_to_diagonal(acc):
    b, c = acc.shape
    blk = lax.broadcasted_iota(jnp.int32, (b, c), 1) // b
    return jnp.concatenate([jnp.where(blk == bi, acc, 0.0) for bi in range(c // b)], axis=0)


def _pair_block_diag(t):
    c = t.shape[0]
    left = lax.broadcasted_iota(jnp.int32, (1, 2 * c), 1) < c
    zero = jnp.zeros_like(t)
    return jnp.concatenate([jnp.where(left, t, zero), jnp.where(left, zero, t)], axis=0)


def _dn_kernel(qf_ref, kf_ref, vf_ref, tf_ref, ttf_ref, qb_ref, kb_ref, vb_ref, tb_ref, ttb_ref,
               of_ref, ob_ref, state_ref, m_ref, attn_ref, x_ref, vbeta_ref, kbeta_ref, qg_ref, kdec_ref, gl_ref):
    step = pl.program_id(0)
    slots = (m_ref, attn_ref, x_ref, vbeta_ref, kbeta_ref, qg_ref, kdec_ref, gl_ref)

    @pl.when(step == 0)
    def _():
        state_ref[...] = jnp.zeros_like(state_ref)
        for ref in slots:
            ref[...] = jnp.zeros_like(ref)

    wr = step % 2
    rd = 1 - wr
    c = DN_CHUNK
    nh = DN_HEADS
    hd_w = DN_HEAD_DIM
    b = DN_BASE
    npairs = nh
    nsub = DN_STEP_CHUNKS
    ri = lax.broadcasted_iota(jnp.int32, (c, 2 * c), 0)
    lane2 = lax.broadcasted_iota(jnp.int32, (c, 2 * c), 1)
    ci = lane2 % c
    left = lane2 < c

    def pair_cols(tab, lane_a):
        return jnp.where(left, tab[:, lane_a:lane_a + 1], tab[:, lane_a + 1:lane_a + 2])

    def pair_slice(p):
        hd = (p % (nh // 2)) * 2
        return slice(hd * hd_w, (hd + 2) * hd_w)

    def table_lanes(p):
        lb = (p // (nh // 2)) * nh + (p % (nh // 2)) * 2
        return lb, 2 * nh + lb

    def prepare():
        ms = {}
        for sub in range(nsub):
            rows = pl.ds(sub * c, c)
            for direction, (q_ref, k_ref, v_ref, t_ref, tt_ref) in enumerate(
                    ((qf_ref, kf_ref, vf_ref, tf_ref, ttf_ref), (qb_ref, kb_ref, vb_ref, tb_ref, ttb_ref))):
                upper = direction == 1
                incl = (ri <= ci) if upper else (ri >= ci)
                strict = (ri < ci) if upper else (ri > ci)
                tab = t_ref[rows, :]
                tabt = tt_ref[:, sub * c:(sub + 1) * c]
                for p in range(direction * nh // 2, (direction + 1) * nh // 2):
                    cs = pair_slice(p)
                    lb, lg = table_lanes(p)
                    kp = k_ref[rows, cs]
                    beta = pair_cols(tab, lb)
                    gcol = pair_cols(tab, lg)
                    grow = jnp.concatenate([tabt[lg:lg + 1, :], tabt[lg + 1:lg + 2, :]], axis=1)
                    decay = jnp.exp(jnp.where(incl, gcol - grow, NEG_INF))
                    prod = _dot_nt(jnp.concatenate([q_ref[rows, cs], kp], axis=0), _pair_block_diag(kp))
                    m = jnp.where(strict, prod[c:] * beta * decay, 0.0)
                    m_ref[wr, sub, p] = m.astype(_BF)
                    attn_ref[wr, sub, p] = (prod[:c] * decay).astype(_BF)
                    ms[sub, p] = m
                    yield
                    k = kp.astype(_F32)
                    glast = gcol[0:1, :] if upper else gcol[c - 1:c, :]
                    eg = jnp.exp(gcol)
                    vbeta_ref[wr, sub, p] = (v_ref[rows, cs].astype(_F32) * beta).astype(_BF)
                    kbeta_ref[wr, sub, p] = (k * (beta * eg)).astype(_BF)
                    qg_ref[wr, sub, p] = (q_ref[rows, cs].astype(_F32) * eg).astype(_BF)
                    kdec_ref[wr, sub, p] = (k * jnp.exp(glast - gcol)).astype(_BF)
                    gl_ref[wr, sub, p] = jnp.broadcast_to(jnp.exp(glast), (SUBLANES, 2 * c))
                    yield
        keys = [[(sub, p) for sub in range(nsub) for p in range(d * nh // 2, (d + 1) * nh // 2)] for d in range(2)]
        dgs = [jnp.concatenate([_diag_blocks_by_lane(ms[key][:, h * c:(h + 1) * c]) for key in keys[d] for h in range(2)],
                               axis=0) for d in range(2)]
        nmat = 2 * len(keys[0])
        blk = lax.broadcasted_iota(jnp.int32, (nmat * b, c), 1) // b

        def column(dg, j):
            return jnp.take_along_axis(dg, blk * b + j, axis=1).reshape(nmat, b, c)

        lane = lax.broadcasted_iota(jnp.int32, (b, c), 1)
        sub_i = lax.broadcasted_iota(jnp.int32, (b, c), 0)
        eye = jnp.broadcast_to(jnp.where(sub_i == lane % b, 1.0, 0.0), (nmat, b, c))
        lo, up = eye, eye
        for t in range(b - 1):
            lo = lo - column(dgs[0], t) * lo[:, t:t + 1, :]
            up = up - column(dgs[1], b - 1 - t) * up[:, b - 1 - t:b - t, :]
            if t % 3 == 2:
                yield
        for d, acc in enumerate((lo, up)):
            for n, (sub, p) in enumerate(keys[d]):
                x_ref[wr, sub, p] = jnp.concatenate(
                    [_blocks_to_diagonal(acc[2 * n]), _blocks_to_diagonal(acc[2 * n + 1])], axis=1).astype(_BF)
            yield

    def recur(group, turn):
        at = {p: (turn if p < npairs // 2 else nsub - 1 - turn) for p in group}
        ms = {p: m_ref[rd, at[p], p] for p in group}
        xs = {p: x_ref[rd, at[p], p] for p in group}
        size = b
        one = jnp.ones((), _BF)
        while size < c:
            off_diag = (ri // (2 * size) == ci // (2 * size)) & (ri // size != ci // size)
            zero = jnp.zeros_like(ms[group[0]])
            ts = {p: _dot(xs[p], _pair_block_diag(jnp.where(off_diag, ms[p], zero))).astype(_BF) for p in group}
            yield
            xs = {p: _dot(jnp.where(ri == ci, one, -ts[p]), _pair_block_diag(xs[p])).astype(_BF) for p in group}
            yield
            size *= 2
        us, wqs = {}, {}
        for p in group:
            vbeta, kbeta = vbeta_ref[rd, at[p], p], kbeta_ref[rd, at[p], p]
            sols = [_dot(xs[p][:, h * c:(h + 1) * c],
                         jnp.concatenate([vbeta[:, h * hd_w:(h + 1) * hd_w], kbeta[:, h * hd_w:(h + 1) * hd_w]], axis=1))
                    for h in range(2)]
            us[p] = jnp.concatenate([sols[0][:, :hd_w], sols[1][:, :hd_w]], axis=1)
            w = jnp.concatenate([sols[0][:, hd_w:], sols[1][:, hd_w:]], axis=1)
            wqs[p] = jnp.concatenate([w.astype(_BF), qg_ref[rd, at[p], p]], axis=0)
            yield
        states = {p: state_ref[p] for p in group}
        wss = {p: _dot(wqs[p], _pair_block_diag(states[p].astype(_BF))) for p in group}
        yield
        vbs = {p: (us[p] - wss[p][:c]).astype(_BF) for p in group}
        for p in group:
            o_ref = of_ref if p < npairs // 2 else ob_ref
            o_ref[pl.ds(at[p] * c, c), pair_slice(p)] = (
                wss[p][c:] + _dot(attn_ref[rd, at[p], p], _pair_block_diag(vbs[p])))
        yield
        for p in group:
            kdec = kdec_ref[rd, at[p], p]
            upd = [_dot_tn(kdec[:, h * hd_w:(h + 1) * hd_w], vbs[p][:, h * hd_w:(h + 1) * hd_w]) for h in range(2)]
            state_ref[p] = states[p] * gl_ref[rd, at[p], p, 0:1, :] + jnp.concatenate(upd, axis=1)
        yield

    groups = [list(range(g, g + DN_GROUP)) for g in range(0, npairs, DN_GROUP)]
    recurrence = itertools.chain(*(recur(g, turn) for turn in range(nsub) for g in groups))
    for _ in itertools.zip_longest(recurrence, prepare()):
        pass


def _gated_delta(qd, kd, vd, tab, tabt):
    s = qd.shape[0]
    c = DN_CHUNK
    nsub = DN_STEP_CHUNKS
    rows = nsub * c
    n = s // rows
    npairs = DN_HEADS
    def block_of(stream, flip):
        def index(i):
            j = jnp.maximum(i - 1, 0) if stream == "rec" else jnp.minimum(i, n - 1)
            return n - 1 - j if flip else j
        return index

    def wide(stream, flip):
        index = block_of(stream, flip)
        return pl.BlockSpec((rows, DN_WIDTH), lambda i: (index(i), 0))

    def table(flip):
        index = block_of("prep", flip)
        return pl.BlockSpec((rows, LANES), lambda i: (index(i), 0))

    def table_t(flip):
        index = block_of("prep", flip)
        return pl.BlockSpec((4 * DN_HEADS, rows), lambda i: (0, index(i)))

    specs = []
    for flip in (False, True):
        specs += [wide("prep", flip), wide("prep", flip), wide("prep", flip), table(flip), table_t(flip)]
    slot = pltpu.VMEM((2, nsub, npairs, c, 2 * c), _BF)
    return pl.pallas_call(
        _dn_kernel,
        grid=(n + 1,),
        in_specs=specs,
        out_specs=[wide("rec", False), wide("rec", True)],
        out_shape=[jax.ShapeDtypeStruct((s, DN_WIDTH), _F32)] * 2,
        scratch_shapes=[pltpu.VMEM((npairs, DN_HEAD_DIM, 2 * DN_HEAD_DIM), _F32),
                        slot, slot, slot,
                        slot, slot, slot, slot,
                        pltpu.VMEM((2, nsub, npairs, SUBLANES, 2 * c), _F32)],
        compiler_params=_cparams("arbitrary"),
    )(qd, kd, vd, tab, tabt, qd, kd, vd, tab, tabt)


def _token_order(blk_ref, stage_ref, dil, row0, rows):
    gw = ATT_GROUP_WIDTH
    src = pl.ds(row0 // dil, rows // dil)
    if dil == 1:
        return blk_ref[src, :].astype(_F32)
    for r in range(dil):
        for c in range(gw // LANES):
            stage_ref[c, pl.ds(row0 + r, rows // dil, stride=dil), :] = (
                blk_ref[src, r * gw + c * LANES:r * gw + (c + 1) * LANES].astype(_F32))
    return jnp.concatenate([stage_ref[c, pl.ds(row0, rows), :] for c in range(gw // LANES)], axis=1)


def _out_kernel(x_ref, o0_ref, o1_ref, o2_ref, l0_ref, l1_ref, l2_ref, of_ref, ob_ref, z_ref, gate_ref, vec_ref,
                wa_ref, wd_ref, wo_ref, out_ref, oatt_ref, odn_ref, y_ref, stage_ref, *, tm):
    d = D_MODEL
    dils = [dil for _, dil in ATT_GROUPS]
    rows = tm // OUT_ROW_SPLIT
    subs = [(r * rows, pl.ds(r * rows, rows)) for r in range(OUT_ROW_SPLIT)]
    for row0, rs in subs:
        o0, o1, o2 = (_token_order(ref, stage_ref.at[n], dils[n], row0, rows)
                      for n, ref in enumerate((o0_ref, o1_ref, o2_ref)))
        l0, l1, l2 = (_token_order(ref, stage_ref.at[3 + n], dils[n], row0, rows)
                      for n, ref in enumerate((l0_ref, l1_ref, l2_ref)))
        mx = jnp.maximum(jnp.maximum(l0, l1), l2)
        e0, e1, e2 = jnp.exp(l0 - mx), jnp.exp(l1 - mx), jnp.exp(l2 - mx)
        oatt_ref[rs, :] = ((e0 * o0 + e1 * o1 + e2 * o2) / (e0 + e1 + e2)).astype(_BF)
        for hd in range(DN_HEADS):
            sl = slice(hd * DN_HEAD_DIM, (hd + 1) * DN_HEAD_DIM)
            o = of_ref[rs, sl] + ob_ref[rs, sl]
            o = _rms(o) * vec_ref[5:6, 0:DN_HEAD_DIM]
            odn_ref[rs, sl] = (o * z_ref[rs, sl].astype(_F32)).astype(_BF)
    for row0, rs in subs:
        ya = _dot(oatt_ref[rs, :], wa_ref[...])
        yd = _dot(odn_ref[rs, :], wd_ref[...])
        y_ref[rs, :] = (gate_ref[rs, :d].astype(_F32) * ya + gate_ref[rs, d:].astype(_F32) * yd).astype(_BF)
    for row0, rs in subs:
        y = _dot(y_ref[rs, :], wo_ref[...])
        out_ref[rs, :] = x_ref[rs, :] + vec_ref[3:4, :] * (_rms(y) * vec_ref[4:5, :])


def _mixer_output(x, att, o_f, o_b, z, gates, vec, w_a, w_d, w_o):
    s, d = x.shape
    tm = min(ROW_TILE, s)
    row = lambda i: (i, 0)
    (o0, l0), (o1, l1), (o2, l2) = att
    gw = ATT_GROUP_WIDTH
    by_class = [pl.BlockSpec((tm // dil, dil * gw), row) for _, dil in ATT_GROUPS]
    return pl.pallas_call(
        functools.partial(_out_kernel, tm=tm),
        grid=(s // tm,),
        in_specs=[pl.BlockSpec((tm, d), row)] + by_class * 2
        + [pl.BlockSpec((tm, DN_WIDTH), row)] * 3 + [pl.BlockSpec((tm, 2 * d), row)]
        + [_resident(vec.shape), _resident(w_a.shape), _resident(w_d.shape), _resident(w_o.shape)],
        out_specs=pl.BlockSpec((tm, d), row),
        out_shape=jax.ShapeDtypeStruct((s, d), _F32),
        scratch_shapes=[pltpu.VMEM((tm, gw), _BF), pltpu.VMEM((tm, DN_WIDTH), _BF), pltpu.VMEM((tm, d), _BF),
                        pltpu.VMEM((6, gw // LANES, tm, LANES), _F32)],
        compiler_params=_cparams("arbitrary"),
    )(x, o0, o1, o2, l0, l1, l2, o_f, o_b, z, gates, vec, w_a, w_d, w_o)


def _vec(rows, d):
    pad = [jnp.zeros((d,), _F32)] * (SUBLANES - len(rows))
    return jnp.stack(list(rows) + pad, axis=0)


def kernel(x, c, positions, w_ada, b_ada, norm_pre, norm_post, ffn_w_in, ffn_w_out, w_in, conv_w, a_log, dt_bias,
           dn_norm_w, w_branch_att, w_branch_dn, w_out):
    b, s, d = x.shape
    assert b == 1 and d == D_MODEL and s % (DN_CHUNK * 16) == 0
    depth = w_ada.shape[0]
    x2 = x.reshape(s, d)
    mod = _ada_modulation(c, w_ada, b_ada)
    cos_t, sin_t = _rope_tables(positions)

    o_dn = 3 * ATT_WIDTH
    o_z = o_dn + 3 * DN_WIDTH
    o_ba = o_z + DN_WIDTH
    o_g = o_ba + 4 * DN_HEADS
    nh = DN_HEADS
    zeros_row = jnp.zeros((LANES,), _F32)

    for l in range(depth):
        def vec_for(sub, extra=None):
            rows = [norm_pre[l, sub], mod[l, sub, 0], mod[l, sub, 1], mod[l, sub, 2], norm_post[l, sub]]
            if extra is not None:
                rows.append(extra)
            return _vec(rows, d)

        x2 = _ffn_sublayer(x2, vec_for(0), ffn_w_in[l, 0].astype(_BF), ffn_w_out[l, 0].astype(_BF))

        wl = w_in[l]
        w_ba = jnp.pad(wl[:, o_ba:o_g], ((0, 0), (0, LANES - 4 * nh))).astype(_BF)
        wts = (wl[:, :o_dn].astype(_BF), wl[:, o_dn:o_z].astype(_BF), wl[:, o_z:o_ba].astype(_BF),
               wl[:, o_g:].astype(_BF), w_ba)
        neg_a = zeros_row.at[2 * nh:4 * nh].set(-jnp.exp(a_log[l].astype(_F32)).reshape(-1))
        dtb = zeros_row.at[2 * nh:4 * nh].set(dt_bias[l].astype(_F32).reshape(-1))
        dec = _vec([neg_a, dtb], LANES)
        qa, ka, va, (qd, kd, vd, z, gates, tab, tabt) = _mixer_projection(
            x2, vec_for(1), cos_t, sin_t, wts, conv_w[l], dec)
        att = [_dilated_attention(qa[g], ka[g], va[g], dil) for g, (_, dil) in enumerate(ATT_GROUPS)]
        o_f, o_b = _gated_delta(qd, kd, vd, tab, tabt)
        x2 = _mixer_output(x2, att, o_f, o_b, z, gates, vec_for(1, jnp.tile(dn_norm_w[l], d // DN_HEAD_DIM)),
                           w_branch_att[l].astype(_BF), w_branch_dn[l].astype(_BF), w_out[l].astype(_BF))

        x2 = _ffn_sublayer(x2, vec_for(2), ffn_w_in[l, 1].astype(_BF), ffn_w_out[l, 1].astype(_BF))
    return x2.reshape(b, s, d)
```
